```python
import jax, jax.numpy as jnp
from jax import lax
import numpy as np

D_MODEL = 1024
BATCH = 1
SEQ = 16384
DEPTH = 2
DEC_BATCH = 32
DEC_SEQ = 8
PAST_LEN = 16384
PAGE_SIZE = 128

HEAD_DIM = 64
D_CONV = D_MODEL // 2
H_DSA = (D_MODEL // 2) // HEAD_DIM
D_DSA = H_DSA * HEAD_DIM
H_IDX = 8
D_IDX = 64
DSA_TOPK_MAX = 256
DSA_Q_BLOCK = 128
H_MOBA = D_MODEL // HEAD_DIM
D_MOBA = H_MOBA * HEAD_DIM
MOBA_BLOCK = 256
MOBA_TOPK = 3
MOBA_Q_BLOCK = 64
CONV_W = 3
D_FF = 2816
D_PLE = 256
DEEPNORM_ALPHA = (2 * DEPTH) ** 0.25
DEEPNORM_BETA = (8 * DEPTH) ** -0.25
LN_EPS = 1e-5
SPLIT0_SIZES = (D_CONV, D_CONV, D_CONV, D_DSA, D_DSA, D_DSA, H_IDX * D_IDX, D_IDX, H_IDX)
D_IN0 = 3 * D_CONV + 3 * D_DSA + H_IDX * D_IDX + D_IDX + H_IDX

kernel_name = 'hybrid_conv_dsa_moba_decoder_step'


def layer_norm(x, g, b):
    xf = x.astype(jnp.float32)
    mu = jnp.mean(xf, axis=-1, keepdims=True)
    var = jnp.mean(jnp.square(xf - mu), axis=-1, keepdims=True)
    return ((xf - mu) * lax.rsqrt(var + LN_EPS)).astype(x.dtype) * g + b


def causal_dwconv(u, prev, w):
    t = u.shape[1]
    full = jnp.concatenate([prev.astype(u.dtype), u], axis=1)
    y = full[:, 0:t] * w[0]
    for j in range(1, CONV_W):
        y = y + full[:, j:j + t] * w[j]
    return y, full[:, t:]


def indexer_scores(q_idx, k_idx, w_idx):
    s = jnp.einsum('bqhd,bld->bqhl', q_idx.astype(jnp.float32), k_idx.astype(jnp.float32))
    return jnp.einsum('bqh,bqhl->bql', w_idx.astype(jnp.float32), jax.nn.relu(s)) * (H_IDX * D_IDX) ** -0.5


def gathered_attention(q, ks, vs, valid):
    s = jnp.einsum('bqhd,bqkhd->bqhk', q, ks).astype(jnp.float32) * HEAD_DIM ** -0.5
    s = jnp.where(valid[:, :, None, :], s, -jnp.inf)
    p = jax.nn.softmax(s, axis=-1).astype(vs.dtype)
    return jnp.einsum('bqhk,bqkhd->bqhd', p, vs)


def dsa_prompt(q, k, v, q_idx, k_idx, w_idx):
    b, s = q.shape[:2]
    topk = min(DSA_TOPK_MAX, s // 4)
    nqb = s // DSA_Q_BLOCK
    key_pos = jnp.arange(s)
    bidx = jnp.arange(b)[:, None, None]

    def to_blocks(a):
        return jnp.swapaxes(a.reshape(b, nqb, DSA_Q_BLOCK, *a.shape[2:]), 0, 1)

    def one_block(args):
        qb, qib, wb, qpos = args
        sc = indexer_scores(qib, k_idx, wb)
        sc = jnp.where(key_pos[None, None, :] <= qpos[None, :, None], sc, -jnp.inf)
        _, sel = lax.top_k(sc, topk)
        valid = sel <= qpos[None, :, None]
        return gathered_attention(qb, k[bidx, sel], v[bidx, sel], valid)

    qpos_blocks = jnp.arange(s).reshape(nqb, DSA_Q_BLOCK)
    out = lax.map(one_block, (to_blocks(q), to_blocks(q_idx), to_blocks(w_idx), qpos_blocks))
    return jnp.swapaxes(out, 0, 1).reshape(b, s, H_DSA, HEAD_DIM)


def dsa_sample(q, k_new, v_new, q_idx, k_idx_new, w_idx, cache_k, cache_v, cache_kidx, page_table):
    db, t = q.shape[:2]
    n_keys = PAST_LEN + t
    topk = min(DSA_TOPK_MAX, n_keys // 4)
    bidx = jnp.arange(db)[:, None, None]
    k_idx_past = cache_kidx[page_table].reshape(db, PAST_LEN, D_IDX)
    k_idx_all = jnp.concatenate([k_idx_past, k_idx_new.astype(k_idx_past.dtype)], axis=1)
    qpos = PAST_LEN + jnp.arange(t)
    sc = indexer_scores(q_idx, k_idx_all, w_idx)
    sc = jnp.where(jnp.arange(n_keys)[None, None, :] <= qpos[None, :, None], sc, -jnp.inf)
    _, sel = lax.top_k(sc, topk)
    valid = sel <= qpos[None, :, None]
    from_cache = (sel < PAST_LEN)[..., None, None]
    past = jnp.minimum(sel, PAST_LEN - 1)
    phys = page_table[bidx, past // PAGE_SIZE]
    slot = past % PAGE_SIZE
    cur = jnp.clip(sel - PAST_LEN, 0, t - 1)
    ks = jnp.where(from_cache, cache_k[phys, slot], k_new[bidx, cur])
    vs = jnp.where(from_cache, cache_v[phys, slot], v_new[bidx, cur])
    return gathered_attention(q, ks, vs, valid)


def moba_prompt(q, k, v):
    b, s = q.shape[:2]
    nb = -(-s // MOBA_BLOCK)
    topk = min(MOBA_TOPK, nb - 1)
    pad = ((0, 0), (0, nb * MOBA_BLOCK - s), (0, 0), (0, 0))
    kp, vp = jnp.pad(k, pad), jnp.pad(v, pad)
    k_blocks = jnp.swapaxes(kp.reshape(b, nb, MOBA_BLOCK, H_MOBA, HEAD_DIM), 2, 3)
    v_blocks = jnp.swapaxes(vp.reshape(b, nb, MOBA_BLOCK, H_MOBA, HEAD_DIM), 2, 3)
    k_mean = jnp.mean(k_blocks.astype(jnp.float32), axis=3)
    bidx = jnp.arange(b)[:, None, None, None]
    hidx = jnp.arange(H_MOBA)[None, None, :, None]
    scale = HEAD_DIM ** -0.5
    nqb = s // MOBA_Q_BLOCK

    def one_block(args):
        qb, qpos = args
        own = qpos[0] // MOBA_BLOCK
        start = own * MOBA_BLOCK
        k_own = lax.dynamic_slice_in_dim(kp, start, MOBA_BLOCK, axis=1)
        v_own = lax.dynamic_slice_in_dim(vp, start, MOBA_BLOCK, axis=1)
        own_pos = start + jnp.arange(MOBA_BLOCK)
        s_own = jnp.einsum('bqhd,bkhd->bqhk', qb, k_own).astype(jnp.float32) * scale
        s_own = jnp.where((own_pos[None, :] <= qpos[:, None])[None, :, None, :], s_own, -jnp.inf)
        if topk == 0:
            p_own = jax.nn.softmax(s_own, axis=-1).astype(v_own.dtype)
            return jnp.einsum('bqhk,bkhd->bqhd', p_own, v_own)
        gate = jnp.einsum('bqhd,bnhd->bqhn', qb.astype(jnp.float32), k_mean)
        gate = jnp.where(jnp.arange(nb) < own, gate, -jnp.inf)
        _, sel = lax.top_k(gate, topk)
        ks = k_blocks[bidx, sel, hidx]
        vs = v_blocks[bidx, sel, hidx]
        s_sel = jnp.einsum('bqhd,bqhnkd->bqhnk', qb, ks).astype(jnp.float32) * scale
        s_sel = jnp.where((sel < own)[..., None], s_sel, -jnp.inf)
        n_sel = topk * MOBA_BLOCK
        p = jax.nn.softmax(jnp.concatenate([s_sel.reshape(b, MOBA_Q_BLOCK, H_MOBA, n_sel), s_own], axis=-1), axis=-1).astype(vs.dtype)
        p_sel = p[..., :n_sel].reshape(s_sel.shape)
        return jnp.einsum('bqhnk,bqhnkd->bqhd', p_sel, vs) + jnp.einsum('bqhk,bkhd->bqhd', p[..., n_sel:], v_own)

    q_blocks = jnp.swapaxes(q.reshape(b, nqb, MOBA_Q_BLOCK, H_MOBA, HEAD_DIM), 0, 1)
    qpos_blocks = jnp.arange(s).reshape(nqb, MOBA_Q_BLOCK)
    out = lax.map(one_block, (q_blocks, qpos_blocks))
    return jnp.swapaxes(out, 0, 1).reshape(b, s, H_MOBA, HEAD_DIM)


def moba_sample(q, k_new, v_new, cache_k, cache_v, page_table):
    assert DEC_SEQ <= PAGE_SIZE and MOBA_BLOCK % PAGE_SIZE == 0
    t = q.shape[1]
    n_pages = PAST_LEN // PAGE_SIZE
    pages_per_block = MOBA_BLOCK // PAGE_SIZE
    n_past_blocks = PAST_LEN // MOBA_BLOCK
    own_past_pages = (PAST_LEN % MOBA_BLOCK) // PAGE_SIZE
    own_start = n_past_blocks * MOBA_BLOCK
    topk = min(MOBA_TOPK, n_past_blocks)
    qpos = PAST_LEN + jnp.arange(t)
    hidx = jnp.arange(H_MOBA)[:, None, None, None]
    scale = HEAD_DIM ** -0.5

    def one_seq(args):
        qs, kn, vn, pt = args
        if own_past_pages > 0:
            own_pages = pt[n_pages - own_past_pages:]
            k_own = jnp.concatenate([cache_k[own_pages].reshape(own_past_pages * PAGE_SIZE, H_MOBA, HEAD_DIM), kn], axis=0)
            v_own = jnp.concatenate([cache_v[own_pages].reshape(own_past_pages * PAGE_SIZE, H_MOBA, HEAD_DIM), vn], axis=0)
        else:
            k_own, v_own = kn, vn
        own_pos = own_start + jnp.arange(k_own.shape[0])
        s_own = jnp.einsum('thd,khd->thk', qs, k_own).astype(jnp.float32) * scale
        s_own = jnp.where((own_pos[None, :] <= qpos[:, None])[:, None, :], s_own, -jnp.inf)
        if topk == 0:
            p_own = jax.nn.softmax(s_own, axis=-1).astype(v_own.dtype)
            return jnp.einsum('thk,khd->thd', p_own, v_own)
        block_pages = pt[:n_past_blocks * pages_per_block].reshape(n_past_blocks, pages_per_block)
        k_mean = jnp.mean(cache_k[block_pages].astype(jnp.float32), axis=(1, 2))
        gate = jnp.einsum('thd,nhd->thn', qs.astype(jnp.float32), k_mean)
        _, sel = lax.top_k(gate, topk)
        phys = block_pages[sel][..., None]
        slots = jnp.arange(PAGE_SIZE)
        ks = cache_k[phys, slots, hidx].reshape(t, H_MOBA, topk, MOBA_BLOCK, HEAD_DIM)
        vs = cache_v[phys, slots, hidx].reshape(t, H_MOBA, topk, MOBA_BLOCK, HEAD_DIM)
        s_sel = jnp.einsum('thd,thnkd->thnk', qs, ks).astype(jnp.float32) * scale
        n_sel = topk * MOBA_BLOCK
        p = jax.nn.softmax(jnp.concatenate([s_sel.reshape(t, H_MOBA, n_sel), s_own], axis=-1), axis=-1).astype(vs.dtype)
        p_sel = p[..., :n_sel].reshape(s_sel.shape)
        return jnp.einsum('thnk,thnkd->thd', p_sel, vs) + jnp.einsum('thk,khd->thd', p[..., n_sel:], v_own)

    return lax.map(one_seq, (q, k_new, v_new, page_table))


def mixer_ab(x, conv_prev, attn_fn, w_in0, w_conv_a0, w_out0):
    b, t = x.shape[:2]
    offs = []
    acc = 0
    for n in SPLIT0_SIZES[:-1]:
        acc += n
        offs.append(acc)
    bg, cg, hv, q, k, v, qi, ki, wi = jnp.split(x @ w_in0, offs, axis=-1)
    y_conv, conv_new = causal_dwconv(cg * hv, conv_prev, w_conv_a0)
    y_a = bg * y_conv
    q, k, v = [a.reshape(b, t, H_DSA, HEAD_DIM) for a in (q, k, v)]
    qi = qi.reshape(b, t, H_IDX, D_IDX)
    y_b = attn_fn(q, k, v, qi, ki, wi).reshape(b, t, D_DSA)
    return jnp.concatenate([y_a, y_b], axis=-1) @ w_out0, k, v, ki, conv_new


def mixer_c(x, attn_fn, w_in1, w_out1):
    b, t = x.shape[:2]
    q, k, v = [a.reshape(b, t, H_MOBA, HEAD_DIM) for a in jnp.split(x @ w_in1, 3, axis=-1)]
    return attn_fn(q, k, v).reshape(b, t, D_MOBA) @ w_out1, k, v


def convffn_block(x, prev, w_in, w_conv, w_out):
    u, g = jnp.split(x @ w_in, 2, axis=-1)
    uc, new_state = causal_dwconv(u, prev, w_conv)
    return (jax.nn.gelu(uc) * g) @ w_out, new_state


def run_trunk(x, p, conv_prev, ffn_prev, attn_ab, attn_c, w_in0, w_conv_a0, w_out0, w_in1, w_out1,
              ln_mix_g, ln_mix_b, ln_ffn_g, ln_ffn_b, w_ffn_in, w_conv_ffn, w_ffn_out, w_ple, w_ple_gate):
    ffn_states = []
    for i in range(DEPTH):
        if i % 2 == 0:
            y, k0, v0, ki0, conv_new = mixer_ab(x, conv_prev, attn_ab, w_in0, w_conv_a0, w_out0)
        else:
            y, k1, v1 = mixer_c(x, attn_c, w_in1, w_out1)
        x = layer_norm(DEEPNORM_ALPHA * x + y, ln_mix_g[i], ln_mix_b[i])
        f, st = convffn_block(x, ffn_prev[i], w_ffn_in[i], w_conv_ffn[i], w_ffn_out[i])
        ffn_states.append(st)
        x = layer_norm(DEEPNORM_ALPHA * x + f, ln_ffn_g[i], ln_ffn_b[i])
        x = x + jax.nn.sigmoid(x @ w_ple_gate[i]) * (p[i] @ w_ple[i])
    return x, k0, v0, ki0, conv_new, k1, v1, jnp.stack(ffn_states)


def _normal(key, shape, scale=1.0):
    return jax.random.normal(key, shape, jnp.float32) * scale


def setup_inputs(seed: int = 0) -> dict:
    key = jax.random.key(seed)
    ks = jax.random.split(key, 32)
    n_pages = PAST_LEN // PAGE_SIZE
    n_pool = (5 * DEC_BATCH * n_pages + 3) // 4
    page_table = jax.random.permutation(ks[0], n_pool)[:DEC_BATCH * n_pages].reshape(DEC_BATCH, n_pages).astype(jnp.int32)
    return {
        'x_prompt': _normal(ks[1], (BATCH, SEQ, D_MODEL)),
        'x_sample': _normal(ks[2], (DEC_BATCH, DEC_SEQ, D_MODEL)),
        'cache_k0': _normal(ks[3], (n_pool, PAGE_SIZE, H_DSA, HEAD_DIM)),
        'cache_v0': _normal(ks[4], (n_pool, PAGE_SIZE, H_DSA, HEAD_DIM)),
        'cache_kidx0': _normal(ks[5], (n_pool, PAGE_SIZE, D_IDX)),
        'state_conv0': _normal(ks[6], (DEC_BATCH, CONV_W - 1, D_CONV)),
        'cache_k1': _normal(ks[7], (n_pool, PAGE_SIZE, H_MOBA, HEAD_DIM)),
        'cache_v1': _normal(ks[8], (n_pool, PAGE_SIZE, H_MOBA, HEAD_DIM)),
        'state_ffn': _normal(ks[9], (DEPTH, DEC_BATCH, CONV_W - 1, D_FF)),
        'page_table': page_table,
        'p_prompt': _normal(ks[10], (DEPTH, BATCH, SEQ, D_PLE)),
        'p_sample': _normal(ks[11], (DEPTH, DEC_BATCH, DEC_SEQ, D_PLE)),
        'w_in0': _normal(ks[12], (D_MODEL, D_IN0), D_MODEL ** -0.5),
        'w_conv_a0': _normal(ks[13], (CONV_W, D_CONV), 0.5),
        'w_out0': _normal(ks[14], (D_CONV + D_DSA, D_MODEL), (D_CONV + D_DSA) ** -0.5 * DEEPNORM_BETA),
        'w_in1': _normal(ks[15], (D_MODEL, 3 * D_MOBA), D_MODEL ** -0.5),
        'w_out1': _normal(ks[16], (D_MOBA, D_MODEL), D_MOBA ** -0.5 * DEEPNORM_BETA),
        'ln_mix_g': 1.0 + _normal(ks[17], (DEPTH, D_MODEL), 0.02),
        'ln_mix_b': _normal(ks[18], (DEPTH, D_MODEL), 0.02),
        'ln_ffn_g': 1.0 + _normal(ks[19], (DEPTH, D_MODEL), 0.02),
        'ln_ffn_b': _normal(ks[20], (DEPTH, D_MODEL), 0.02),
        'w_ffn_in': _normal(ks[21], (DEPTH, D_MODEL, 2 * D_FF), D_MODEL ** -0.5),
        'w_conv_ffn': _normal(ks[22], (DEPTH, CONV_W, D_FF), 0.5),
        'w_ffn_out': _normal(ks[23], (DEPTH, D_FF, D_MODEL), D_FF ** -0.5 * DEEPNORM_BETA),
        'w_ple': _normal(ks[24], (DEPTH, D_PLE, D_MODEL), D_PLE ** -0.5),
        'w_ple_gate': _normal(ks[25], (DEPTH, D_MODEL, D_MODEL), D_MODEL ** -0.5),
    }


def reference(x_prompt, x_sample, cache_k0, cache_v0, cache_kidx0, state_conv0, cache_k1, cache_v1, state_ffn,
              page_table, p_prompt, p_sample, w_in0, w_conv_a0, w_out0, w_in1, w_out1,
              ln_mix_g, ln_mix_b, ln_ffn_g, ln_ffn_b, w_ffn_in, w_conv_ffn, w_ffn_out, w_ple, w_ple_gate):
    b = x_prompt.shape[0]
    conv_zero = jnp.zeros((b, CONV_W - 1, D_CONV), x_prompt.dtype)
    ffn_zero = jnp.zeros((DEPTH, b, CONV_W - 1, D_FF), x_prompt.dtype)
    y_prompt, k0_p, v0_p, kidx0_p, conv0_p, k1_p, v1_p, ffn_p = run_trunk(
        x_prompt, p_prompt, conv_zero, ffn_zero, dsa_prompt, moba_prompt,
        w_in0, w_conv_a0, w_out0, w_in1, w_out1, ln_mix_g, ln_mix_b, ln_ffn_g, ln_ffn_b,
        w_ffn_in, w_conv_ffn, w_ffn_out, w_ple, w_ple_gate)

    def attn_ab_sample(q, k, v, qi, ki, wi):
        return dsa_sample(q, k, v, qi, ki, wi, cache_k0, cache_v0, cache_kidx0, page_table)

    def attn_c_sample(q, k, v):
        return moba_sample(q, k, v, cache_k1, cache_v1, page_table)

    y_sample, k0_s, v0_s, kidx0_s, conv0_s, k1_s, v1_s, ffn_s = run_trunk(
        x_sample, p_sample, state_conv0, state_ffn, attn_ab_sample, attn_c_sample,
        w_in0, w_conv_a0, w_out0, w_in1, w_out1, ln_mix_g, ln_mix_b, ln_ffn_g, ln_ffn_b,
        w_ffn_in, w_conv_ffn, w_ffn_out, w_ple, w_ple_gate)
    return (y_prompt, y_sample, k0_p, v0_p, kidx0_p, conv0_p, k1_p, v1_p, ffn_p,
            k0_s, v0_s, kidx0_s, conv0_s, k1_s, v1_s, ffn_s)
```

```python
import functools

import jax
import jax.numpy as jnp
from jax import lax
from jax.experimental import pallas as pl
from jax.experimental.pallas import tpu as pltpu

HEAD_DIM = 64
H_IDX = 8
D_IDX = 64
DSA_TOPK_MAX = 256
MOBA_BLOCK = 256
MOBA_TOPK = 3
CONV_W = 3
LN_EPS = 1e-5

LANES = 128
SUBLANES = 8
VMEM_LIMIT_BYTES = 56 * 1024 * 1024
ROW_TILE = 256
DSA_TQ = 128
DSA_KC = 512
SAMPLE_PAGES_PER_CHUNK = 4

MXU_DTYPE = jnp.bfloat16
INT_MIN = -(2 ** 31)
NEG_INF = float("-inf")
M_INIT = -1e30


def _cparams(semantics):
    return pltpu.CompilerParams(dimension_semantics=semantics,
                                vmem_limit_bytes=VMEM_LIMIT_BYTES)


def _resident(shape):
    nd = len(shape)
    return pl.BlockSpec(shape, lambda *_: (0,) * nd, pipeline_mode=pl.Buffered(1))


def _dot(a, b):
    return jnp.dot(a, b, preferred_element_type=jnp.float32)


def _dot_nt(a, b):
    return lax.dot_general(a, b, (((1,), (1,)), ((), ())),
                           preferred_element_type=jnp.float32)


def _layer_norm(z, g, b):
    mu = jnp.mean(z, axis=-1, keepdims=True)
    var = jnp.mean(jnp.square(z - mu), axis=-1, keepdims=True)
    return (z - mu) * lax.rsqrt(var + LN_EPS) * g + b


def _conv3_carry(u, w_ref, ubuf_ref, prev_ref, first_tile):
    tm = u.shape[0]

    @pl.when(first_tile)
    def _():
        ubuf_ref[SUBLANES - 2:SUBLANES, :] = prev_ref[0]

    ubuf_ref[SUBLANES:SUBLANES + tm, :] = u
    u1 = ubuf_ref[SUBLANES - 1:SUBLANES - 1 + tm, :]
    u2 = ubuf_ref[SUBLANES - 2:SUBLANES - 2 + tm, :]
    y = u2 * w_ref[0:1, :] + u1 * w_ref[1:2, :] + u * w_ref[2:3, :]
    ubuf_ref[0:SUBLANES, :] = ubuf_ref[tm:tm + SUBLANES, :]
    return y


def _conv3_rows(u, w_ref, ubuf_ref, p1_ref, p2_ref, tpos_ref):
    tm = u.shape[0]
    ubuf_ref[0:SUBLANES, :] = jnp.zeros((SUBLANES, u.shape[1]), jnp.float32)
    ubuf_ref[SUBLANES:SUBLANES + tm, :] = u
    t = tpos_ref[...]
    u1 = jnp.where(t >= 1, ubuf_ref[SUBLANES - 1:SUBLANES - 1 + tm, :], p1_ref[...])
    u2 = jnp.where(t >= 2, ubuf_ref[SUBLANES - 2:SUBLANES - 2 + tm, :], p2_ref[...])
    return u2 * w_ref[0:1, :] + u1 * w_ref[1:2, :] + u * w_ref[2:3, :]


def _conv_prev_rows(prev, t_len):
    b, _, c = prev.shape
    z = jnp.zeros((b, t_len, c), jnp.float32)
    p1 = z.at[:, 0].set(prev[:, 1])
    p2 = z.at[:, 0].set(prev[:, 0]).at[:, 1].set(prev[:, 1])
    return p1.reshape(b * t_len, c), p2.reshape(b * t_len, c)


def _tpos(b, t_len):
    return jnp.tile(jnp.arange(t_len, dtype=jnp.int32), b).reshape(b * t_len, 1)


def _proj0_body(carry_mode, c, dd, x_ref, w_ref, wc_ref, *rest):
    if carry_mode:
        (prev_ref, ya_ref, qb_ref, k_ref, kb_ref, v_ref, vb_ref, qib_ref, ki_ref,
         kib2_ref, wi_ref, st_ref, ubuf_ref) = rest
    else:
        (p1_ref, p2_ref, tpos_ref, ya_ref, qb_ref, k_ref, kb_ref, v_ref, vb_ref,
         qib_ref, ki_ref, kib2_ref, wi_ref, u_ref, ubuf_ref) = rest
    res = _dot(x_ref[...].astype(MXU_DTYPE), w_ref[...])
    bg = res[:, 0:c]
    u = res[:, c:2 * c] * res[:, 2 * c:3 * c]
    if carry_mode:
        y = _conv3_carry(u, wc_ref, ubuf_ref, prev_ref, pl.program_id(1) == 0)
        st_ref[0] = u[u.shape[0] - 2:, :]
    else:
        y = _conv3_rows(u, wc_ref, ubuf_ref, p1_ref, p2_ref, tpos_ref)
        u_ref[...] = u
    ya_ref[...] = bg * y
    o = 3 * c
    q = res[:, o:o + dd]
    k = res[:, o + dd:o + 2 * dd]
    v = res[:, o + 2 * dd:o + 3 * dd]
    qb_ref[...] = (q * HEAD_DIM ** -0.5).astype(MXU_DTYPE)
    k_ref[...] = k
    kb_ref[...] = k.astype(MXU_DTYPE)
    v_ref[...] = v
    vb_ref[...] = v.astype(MXU_DTYPE)
    o += 3 * dd
    di = H_IDX * D_IDX
    qib_ref[...] = res[:, o:o + di].astype(MXU_DTYPE)
    o += di
    ki_ref[...] = res[:, o:o + D_IDX]
    kib2_ref[...] = res[:, o:o + 2 * D_IDX].astype(MXU_DTYPE)
    o += 2 * D_IDX
    wi_ref[...] = res[:, o:o + H_IDX]


def _proj0(x, w_pad, w_conv, prev, c, dd):
    b, t, d = x.shape
    m = b * t
    npad = w_pad.shape[1]
    di = H_IDX * D_IDX
    carry_mode = t % ROW_TILE == 0
    x2 = x.reshape(m, d)
    f32, mx = jnp.float32, MXU_DTYPE
    row_shapes = [(c, f32), (dd, mx), (dd, f32), (dd, mx), (dd, f32), (dd, mx),
                  (di, mx), (D_IDX, f32), (2 * D_IDX, mx), (H_IDX, f32)]
    if carry_mode:
        tm = ROW_TILE
        nt = t // tm
        grid = (b, nt)
        row = lambda n: pl.BlockSpec((tm, n), lambda bi, j: (bi * nt + j, 0))
        in_specs = [row(d), _resident((d, npad)), _resident((CONV_W, c)),
                    pl.BlockSpec((1, CONV_W - 1, c), lambda bi, j: (bi, 0, 0))]
        out_specs = [row(n) for n, _ in row_shapes] + [
            pl.BlockSpec((1, CONV_W - 1, c), lambda bi, j: (bi, 0, 0))]
        out_shape = [jax.ShapeDtypeStruct((m, n), dt) for n, dt in row_shapes] + [
            jax.ShapeDtypeStruct((b, CONV_W - 1, c), f32)]
        args = (x2, w_pad, w_conv, prev)
        sem = ("arbitrary", "arbitrary")
    else:
        assert m % SUBLANES == 0 and t >= CONV_W - 1
        tm = m
        grid = (1,)
        row = lambda n: pl.BlockSpec((tm, n), lambda i: (0, 0))
        p1, p2 = _conv_prev_rows(prev, t)
        in_specs = [row(d), _resident((d, npad)), _resident((CONV_W, c)),
                    row(c), row(c), row(1)]
        out_specs = [row(n) for n, _ in row_shapes] + [row(c)]
        out_shape = [jax.ShapeDtypeStruct((m, n), dt) for n, dt in row_shapes] + [
            jax.ShapeDtypeStruct((m, c), f32)]
        args = (x2, w_pad, w_conv, p1, p2, _tpos(b, t))
        sem = ("arbitrary",)
    outs = pl.pallas_call(
        functools.partial(_proj0_body, carry_mode, c, dd),
        grid=grid, in_specs=in_specs, out_specs=out_specs, out_shape=out_shape,
        scratch_shapes=[pltpu.VMEM((tm + 2 * SUBLANES, c), f32)],
        compiler_params=_cparams(sem), name="proj0",
    )(*args)
    outs = list(outs)
    if not carry_mode:
        outs[-1] = outs[-1].reshape(b, t, c)[:, t - (CONV_W - 1):, :]
    return outs


def _proj1_body(with_mean, dm, x_ref, w_ref, qb_ref, k_ref, kb_ref, v_ref, vb_ref, *rest):
    res = _dot(x_ref[...].astype(MXU_DTYPE), w_ref[...])
    k = res[:, dm:2 * dm]
    v = res[:, 2 * dm:3 * dm]
    qb_ref[...] = (res[:, 0:dm] * HEAD_DIM ** -0.5).astype(MXU_DTYPE)
    k_ref[...] = k
    kb_ref[...] = k.astype(MXU_DTYPE)
    v_ref[...] = v
    vb_ref[...] = v.astype(MXU_DTYPE)
    if with_mean:
        rest[0][0] = jnp.mean(k, axis=0, keepdims=True)


def _proj1(x, w, dm, with_mean):
    b, t, d = x.shape
    m = b * t
    f32, mx = jnp.float32, MXU_DTYPE
    if with_mean:
        assert t % MOBA_BLOCK == 0
        tm = MOBA_BLOCK
    else:
        tm = m
    nt = m // tm
    row = lambda n: pl.BlockSpec((tm, n), lambda i: (i, 0))
    out_specs = [row(dm)] * 5
    out_shape = [jax.ShapeDtypeStruct((m, dm), dt) for dt in (mx, f32, mx, f32, mx)]
    if with_mean:
        out_specs.append(pl.BlockSpec((1, 1, dm), lambda i: (i, 0, 0)))
        out_shape.append(jax.ShapeDtypeStruct((nt, 1, dm), f32))
    return pl.pallas_call(
        functools.partial(_proj1_body, with_mean, dm),
        grid=(nt,), in_specs=[row(d), _resident((d, 3 * dm))],
        out_specs=out_specs, out_shape=out_shape,
        compiler_params=_cparams(("arbitrary",)), name="proj1",
    )(x.reshape(m, d), w)


def _out_ln_body(n_act, alpha, x_ref, *rest):
    acts = rest[:n_act]
    ws = rest[n_act:2 * n_act]
    g_ref, b_ref, o_ref = rest[2 * n_act:]
    y = _dot(acts[0][...].astype(MXU_DTYPE), ws[0][...])
    for a_ref, w_ref in zip(acts[1:], ws[1:]):
        y = y + _dot(a_ref[...].astype(MXU_DTYPE), w_ref[...])
    o_ref[...] = _layer_norm(alpha * x_ref[...] + y, g_ref[...], b_ref[...])


def _out_ln(x2, acts, ws, g, bta, alpha):
    m, d = x2.shape
    tm = ROW_TILE if m % ROW_TILE == 0 else m
    row = lambda n: pl.BlockSpec((tm, n), lambda i: (i, 0))
    in_specs = ([row(d)] + [row(a.shape[1]) for a in acts]
                + [_resident(w.shape) for w in ws] + [_resident((1, d))] * 2)
    return pl.pallas_call(
        functools.partial(_out_ln_body, len(acts), alpha),
        grid=(m // tm,), in_specs=in_specs, out_specs=row(d),
        out_shape=jax.ShapeDtypeStruct((m, d), jnp.float32),
        compiler_params=_cparams(("arbitrary",)), name="out_ln",
    )(x2, *acts, *ws, g.reshape(1, d), bta.reshape(1, d))


def _ffn_body(carry_mode, f, alpha, x_ref, p_ref, win_ref, wc_ref, wout_ref, g_ref, b_ref,
              wg_ref, wp_ref, *rest):
    if carry_mode:
        prev_ref, o_ref, st_ref, ubuf_ref = rest
    else:
        p1_ref, p2_ref, tpos_ref, o_ref, u_ref, ubuf_ref = rest
    x = x_ref[...]
    h = _dot(x.astype(MXU_DTYPE), win_ref[...])
    u = h[:, 0:f]
    if carry_mode:
        uc = _conv3_carry(u, wc_ref, ubuf_ref, prev_ref, pl.program_id(1) == 0)
        st_ref[0] = u[u.shape[0] - 2:, :]
    else:
        uc = _conv3_rows(u, wc_ref, ubuf_ref, p1_ref, p2_ref, tpos_ref)
        u_ref[...] = u
    a = jax.nn.gelu(uc) * h[:, f:2 * f]
    y = _dot(a.astype(MXU_DTYPE), wout_ref[...])
    x2 = _layer_norm(alpha * x + y, g_ref[...], b_ref[...])
    gate = jax.nn.sigmoid(_dot(x2.astype(MXU_DTYPE), wg_ref[...]))
    o_ref[...] = x2 + gate * _dot(p_ref[...].astype(MXU_DTYPE), wp_ref[...])


def _ffn(x2, p2d, b, t, win, wc, wout, g, bta, wg, wp, prev, alpha):
    m, d = x2.shape
    f = wc.shape[1]
    dp = p2d.shape[1]
    carry_mode = t % ROW_TILE == 0
    f32 = jnp.float32
    weights = [_resident((d, 2 * f)), _resident((CONV_W, f)), _resident((f, d)),
               _resident((1, d)), _resident((1, d)), _resident((d, d)), _resident((dp, d))]
    if carry_mode:
        tm = ROW_TILE
        nt = t // tm
        grid = (b, nt)
        row = lambda n: pl.BlockSpec((tm, n), lambda bi, j: (bi * nt + j, 0))
        st_spec = pl.BlockSpec((1, CONV_W - 1, f), lambda bi, j: (bi, 0, 0))
        in_specs = [row(d), row(dp)] + weights + [st_spec]
        out_specs = [row(d), st_spec]
        out_shape = [jax.ShapeDtypeStruct((m, d), f32),
                     jax.ShapeDtypeStruct((b, CONV_W - 1, f), f32)]
        extra = (prev,)
        sem = ("arbitrary", "arbitrary")
    else:
        tm = m
        grid = (1,)
        row = lambda n: pl.BlockSpec((tm, n), lambda i: (0, 0))
        p1, p2 = _conv_prev_rows(prev, t)
        in_specs = [row(d), row(dp)] + weights + [row(f), row(f), row(1)]
        out_specs = [row(d), row(f)]
        out_shape = [jax.ShapeDtypeStruct((m, d), f32), jax.ShapeDtypeStruct((m, f), f32)]
        extra = (p1, p2, _tpos(b, t))
        sem = ("arbitrary",)
    xo, st = pl.pallas_call(
        functools.partial(_ffn_body, carry_mode, f, alpha),
        grid=grid, in_specs=in_specs, out_specs=out_specs, out_shape=out_shape,
        scratch_shapes=[pltpu.VMEM((tm + 2 * SUBLANES, f), f32)],
        compiler_params=_cparams(sem), name="ffn",
    )(x2, p2d, win, wc, wout, g.reshape(1, d), bta.reshape(1, d), wg, wp, *extra)
    if not carry_mode:
        st = st.reshape(b, t, f)[:, t - (CONV_W - 1):, :]
    return xo, st


def _sortable_key(sc):
    bits = pltpu.bitcast(sc, jnp.int32)
    return bits ^ ((bits >> 31) & 0x7FFFFFFF)


def _count_ge(key_ref, cand, n_chunks, kc):
    rows = key_ref.shape[0]

    def body(c, acc):
        kk = key_ref[:, pl.ds(pl.multiple_of(c * kc, kc), kc)]
        ge = jnp.where(kk >= cand, 1.0, 0.0)
        part = ge[:, 0:LANES]
        for j in range(1, kc // LANES):
            part = part + ge[:, j * LANES:(j + 1) * LANES]
        return acc + part

    acc = lax.fori_loop(0, n_chunks, body, jnp.zeros((rows, LANES), jnp.float32))
    return jnp.sum(acc, axis=1, keepdims=True)


def _select_topk(key_ref, n_chunks, kc, k_sel, kt=None):
    rows = key_ref.shape[0]
    int_min = jnp.int32(INT_MIN)
    kt = kc if kt is None else kt
    assert kc % kt == 0

    def bit_body(i, t_u):
        cand_u = t_u | lax.shift_left(jnp.int32(1), 31 - i)
        cnt = _count_ge(key_ref, cand_u ^ int_min, n_chunks, kc)
        return jnp.where(cnt >= k_sel, cand_u, t_u)

    t_u = lax.fori_loop(0, 32, bit_body, jnp.zeros((rows, 1), jnp.int32))
    thr = jnp.maximum(t_u ^ int_min, int_min + 1)
    cnt_ge = _count_ge(key_ref, thr, n_chunks, kc)
    excess = jnp.max(jnp.where(cnt_ge > k_sel, 1.0, 0.0))

    @pl.when(excess > 0.0)
    def _():
        cnt_gt = _count_ge(key_ref, thr + 1, n_chunks, kc)
        need = k_sel - cnt_gt
        r_i = lax.broadcasted_iota(jnp.int32, (kt, kt), 0)
        c_i = lax.broadcasted_iota(jnp.int32, (kt, kt), 1)
        tri = jnp.where(r_i <= c_i, 1.0, 0.0).astype(MXU_DTYPE)

        def body(c, seen):
            sl = pl.ds(pl.multiple_of(c * kt, kt), kt)
            kk = key_ref[:, sl]
            tie = kk == thr
            tie_f = jnp.where(tie, 1.0, 0.0)
            rank = _dot(tie_f.astype(MXU_DTYPE), tri) + seen
            key_ref[:, sl] = jnp.where(tie & (rank > need), int_min, kk)
            return seen + jnp.sum(tie_f, axis=1, keepdims=True)

        lax.fori_loop(0, n_chunks * (kc // kt), body, jnp.zeros((rows, 1), jnp.float32))

    return thr


def _softmax_step(s, m, l, acc, v, v_is_transposed=False):
    m_new = jnp.maximum(m, jnp.max(s, axis=1, keepdims=True))
    a = jnp.exp(m - m_new)
    p = jnp.exp(s - m_new)
    l_new = a * l + jnp.sum(p, axis=1, keepdims=True)
    pv = (_dot_nt if v_is_transposed else _dot)(p.astype(MXU_DTYPE), v)
    return m_new, l_new, a * acc + pv


def _head_halves(x):
    lane = lax.broadcasted_iota(jnp.int32, x.shape, 1)
    zero = jnp.zeros_like(x)
    return jnp.where(lane < HEAD_DIM, x, zero), jnp.where(lane >= HEAD_DIM, x, zero)


def _dsa_prompt_body(k_sel, qb_ref, kb_ref, vb_ref, qib_ref, kib2_ref, wi_ref, o_ref,
                     key_ref, thr_ref):
    i = pl.program_id(1)
    hp = pl.program_id(2)
    tq = qb_ref.shape[0]
    kc = DSA_KC
    n_chunks = ((i + 1) * tq + kc - 1) // kc
    qpos = i * tq + lax.broadcasted_iota(jnp.int32, (tq, 1), 0)

    @pl.when(hp == 0)
    def _():
        qi = qib_ref[...]
        w = wi_ref[...]
        qm = []
        for hh in range(H_IDX // 2):
            qm.extend(_head_halves(qi[:, hh * LANES:(hh + 1) * LANES]))
        wcol = [w[:, h:h + 1] for h in range(H_IDX)]

        def idx_body(c, carry):
            sl = pl.ds(pl.multiple_of(c * kc, kc), kc)
            kk = kib2_ref[sl, :]
            acc = jnp.zeros((tq, kc), jnp.float32)
            for h in range(H_IDX):
                acc = acc + wcol[h] * jnp.maximum(_dot_nt(qm[h], kk), 0.0)
            sc = acc * (H_IDX * D_IDX) ** -0.5
            kpos = c * kc + lax.broadcasted_iota(jnp.int32, (1, kc), 1)
            key_ref[:, sl] = jnp.where(kpos <= qpos, _sortable_key(sc), jnp.int32(INT_MIN))
            return carry

        lax.fori_loop(0, n_chunks, idx_body, 0)
        thr_ref[...] = _select_topk(key_ref, n_chunks, kc, k_sel)

    thr = thr_ref[...]
    q0, q1 = _head_halves(qb_ref[...])

    def att_body(c, carry):
        m0, l0, a0, m1, l1, a1 = carry
        sl = pl.ds(pl.multiple_of(c * kc, kc), kc)
        kp = kb_ref[sl, :]
        vp = vb_ref[sl, :]
        sel = key_ref[:, sl] >= thr
        s0 = jnp.where(sel, _dot_nt(q0, kp), NEG_INF)
        s1 = jnp.where(sel, _dot_nt(q1, kp), NEG_INF)
        m0, l0, a0 = _softmax_step(s0, m0, l0, a0, vp)
        m1, l1, a1 = _softmax_step(s1, m1, l1, a1, vp)
        return m0, l0, a0, m1, l1, a1

    col = lambda v: jnp.full((tq, 1), v, jnp.float32)
    zacc = jnp.zeros((tq, LANES), jnp.float32)
    init = (col(M_INIT), col(0.0), zacc, col(M_INIT), col(0.0), zacc)
    _, l0, a0, _, l1, a1 = lax.fori_loop(0, n_chunks, att_body, init)
    lane = lax.broadcasted_iota(jnp.int32, (tq, LANES), 1)
    o_ref[...] = jnp.where(lane < HEAD_DIM, a0 / l0, a1 / l1)


def _dsa_prompt(qb, kb, vb, qib, kib2, wi, b, t):
    dd = qb.shape[1]
    k_sel = min(DSA_TOPK_MAX, t // 4)
    tq, kc = DSA_TQ, DSA_KC
    assert t % kc == 0 and kc % tq == 0 and kc >= k_sel and dd % LANES == 0
    nq, n_hp = t // tq, dd // LANES
    r3 = lambda a: a.reshape(b, t, a.shape[1])
    di = qib.shape[1]
    out = pl.pallas_call(
        functools.partial(_dsa_prompt_body, k_sel),
        grid=(b, nq, n_hp),
        in_specs=[
            pl.BlockSpec((None, tq, LANES), lambda bi, i, hp: (bi, i, hp)),
            pl.BlockSpec((None, t, LANES), lambda bi, i, hp: (bi, 0, hp)),
            pl.BlockSpec((None, t, LANES), lambda bi, i, hp: (bi, 0, hp)),
            pl.BlockSpec((None, tq, di), lambda bi, i, hp: (bi, i, 0)),
            pl.BlockSpec((None, t, 2 * D_IDX), lambda bi, i, hp: (bi, 0, 0)),
            pl.BlockSpec((None, tq, H_IDX), lambda bi, i, hp: (bi, i, 0)),
        ],
        out_specs=pl.BlockSpec((None, tq, LANES), lambda bi, i, hp: (bi, i, hp)),
        out_shape=jax.ShapeDtypeStruct((b, t, dd), jnp.float32),
        scratch_shapes=[pltpu.VMEM((tq, t), jnp.int32), pltpu.VMEM((tq, 1), jnp.int32)],
        compiler_params=_cparams(("arbitrary", "arbitrary", "arbitrary")),
        name="dsa_prompt",
    )(r3(qb), r3(kb), r3(vb), r3(qib), r3(kib2), r3(wi))
    return out.reshape(b * t, dd)


def _moba_select(gate, n_valid, topk):
    rows, nb = gate.shape
    n_idx = lax.broadcasted_iota(jnp.int32, (rows, nb), 1)
    valid = n_idx < n_valid
    g = jnp.where(valid, gate, NEG_INF)
    picked = jnp.zeros((rows, nb), jnp.bool_)
    for _ in range(topk):
        mx = jnp.max(g, axis=1, keepdims=True)
        first = jnp.min(jnp.where(g == mx, n_idx, nb), axis=1, keepdims=True)
        hit = n_idx == first
        picked = picked | hit
        g = jnp.where(hit, NEG_INF, g)
    return jnp.where(picked & valid, 1.0, 0.0)


def _moba_prompt_body(topk, qb_ref, kb_ref, vb_ref, km_ref, o_ref):
    i = pl.program_id(2)
    tq = qb_ref.shape[0]
    blk = MOBA_BLOCK
    nb = km_ref.shape[0]
    q0, q1 = _head_halves(qb_ref[...])
    km = km_ref[...].astype(MXU_DTYPE)
    sel0 = _moba_select(_dot_nt(q0, km), i, topk).astype(MXU_DTYPE)
    sel1 = _moba_select(_dot_nt(q1, km), i, topk).astype(MXU_DTYPE)
    n_row = lax.broadcasted_iota(jnp.int32, (nb, blk), 0)

    def past_body(n, carry):
        m0, l0, a0, m1, l1, a1 = carry
        sl = pl.ds(pl.multiple_of(n * blk, blk), blk)
        kp = kb_ref[sl, :]
        vp = vb_ref[sl, :]
        expand = jnp.where(n_row == n, 1.0, 0.0).astype(MXU_DTYPE)
        s0 = jnp.where(_dot(sel0, expand) > 0.5, _dot_nt(q0, kp), NEG_INF)
        s1 = jnp.where(_dot(sel1, expand) > 0.5, _dot_nt(q1, kp), NEG_INF)
        m0, l0, a0 = _softmax_step(s0, m0, l0, a0, vp)
        m1, l1, a1 = _softmax_step(s1, m1, l1, a1, vp)
        return m0, l0, a0, m1, l1, a1

    col = lambda v: jnp.full((tq, 1), v, jnp.float32)
    zacc = jnp.zeros((tq, LANES), jnp.float32)
    init = (col(M_INIT), col(0.0), zacc, col(M_INIT), col(0.0), zacc)
    m0, l0, a0, m1, l1, a1 = lax.fori_loop(0, i, past_body, init)

    sl = pl.ds(pl.multiple_of(i * blk, blk), blk)
    kp = kb_ref[sl, :]
    vp = vb_ref[sl, :]
    causal = (lax.broadcasted_iota(jnp.int32, (tq, blk), 1)
              <= lax.broadcasted_iota(jnp.int32, (tq, blk), 0))
    s0 = jnp.where(causal, _dot_nt(q0, kp), NEG_INF)
    s1 = jnp.where(causal, _dot_nt(q1, kp), NEG_INF)
    _, l0, a0 = _softmax_step(s0, m0, l0, a0, vp)
    _, l1, a1 = _softmax_step(s1, m1, l1, a1, vp)
    lane = lax.broadcasted_iota(jnp.int32, (tq, LANES), 1)
    o_ref[...] = jnp.where(lane < HEAD_DIM, a0 / l0, a1 / l1)


def _moba_prompt(qb, kb, vb, kmean, b, t):
    dm = qb.shape[1]
    blk = MOBA_BLOCK
    assert t % blk == 0 and dm % LANES == 0
    nb = t // blk
    topk = min(MOBA_TOPK, nb - 1)
    r3 = lambda a: a.reshape(b, t, dm)
    km = kmean.reshape(b, nb, dm)
    out = pl.pallas_call(
        functools.partial(_moba_prompt_body, topk),
        grid=(b, dm // LANES, nb),
        in_specs=[
            pl.BlockSpec((None, blk, LANES), lambda bi, hp, i: (bi, i, hp)),
            pl.BlockSpec((None, t, LANES), lambda bi, hp, i: (bi, 0, hp)),
            pl.BlockSpec((None, t, LANES), lambda bi, hp, i: (bi, 0, hp)),
            pl.BlockSpec((None, nb, LANES), lambda bi, hp, i: (bi, 0, hp)),
        ],
        out_specs=pl.BlockSpec((None, blk, LANES), lambda bi, hp, i: (bi, i, hp)),
        out_shape=jax.ShapeDtypeStruct((b, t, dm), jnp.float32),
        compiler_params=_cparams(("arbitrary", "arbitrary", "arbitrary")),
        name="moba_prompt",
    )(r3(qb), r3(kb), r3(vb), km)
    return out.reshape(b * t, dm)


def _chunk_copies(cache_ref, buf_ref, sem, pt_ref, first_page, slot):
    return [pltpu.make_async_copy(cache_ref.at[pt_ref[first_page + p]],
                                  buf_ref.at[slot, p], sem.at[slot])
            for p in range(buf_ref.shape[1])]


def _chunk_transposed(buf_ref, slot):
    cp, nh, dh, page = buf_ref.shape[1:]
    return jnp.concatenate([buf_ref[slot, p].reshape(nh * dh, page) for p in range(cp)],
                           axis=1).astype(MXU_DTYPE)


def _pages_transposed(cache):
    nd = cache.ndim
    return jnp.transpose(cache, (0,) + tuple(range(2, nd)) + (1,))


def _pad_rows(dst_ref, src_ref):
    dst_ref[...] = jnp.zeros(dst_ref.shape, dst_ref.dtype)
    dst_ref[0:src_ref.shape[0], :] = src_ref[...]


def _row_token(n_heads, t_new):
    return lax.broadcasted_iota(jnp.int32, (n_heads, t_new, LANES), 1).reshape(
        n_heads * t_new, LANES)


def _heads_to_lanes(x, n_heads, t_new):
    lane_head = lax.broadcasted_iota(jnp.int32, (t_new, x.shape[1]), 1) // HEAD_DIM
    out = jnp.zeros((t_new, x.shape[1]), jnp.float32)
    for h in range(n_heads):
        out = jnp.where(lane_head == h, x[h * t_new:(h + 1) * t_new, :], out)
    return out


def _dsa_sample_body(k_sel, n_pages, kcs, pt_ref, qblk_ref, qi2_ref, wcol_ref, knew_ref,
                     vnew_ref, kinew_ref, ckidx_ref, ck_ref, cv_ref, o_ref,
                     kidx_buf, key_ref, kbuf, vbuf, knew_pad, vnew_pad, kinew_pad,
                     sem_i, sem_k, sem_v):
    b = pl.program_id(0)
    page0 = b * n_pages
    cp = kbuf.shape[1]
    page = kbuf.shape[4]
    dd = kbuf.shape[2] * kbuf.shape[3]
    n_chunks = n_pages // cp
    past = n_pages * page
    t_new = knew_ref.shape[0]
    n_heads = qblk_ref.shape[0] // t_new
    int_min = jnp.int32(INT_MIN)

    def idx_copy(j):
        return pltpu.make_async_copy(ckidx_ref.at[pt_ref[page0 + j]], kidx_buf.at[j],
                                     sem_i.at[0])

    def kv_copies(j, slot):
        first = page0 + j * cp
        return (_chunk_copies(ck_ref, kbuf, sem_k, pt_ref, first, slot)
                + _chunk_copies(cv_ref, vbuf, sem_v, pt_ref, first, slot))

    def start_idx(j, c):
        idx_copy(j).start()
        return c

    lax.fori_loop(0, n_pages, start_idx, 0)
    for cpy in kv_copies(0, 0):
        cpy.start()
    _pad_rows(knew_pad, knew_ref)
    _pad_rows(vnew_pad, vnew_ref)
    _pad_rows(kinew_pad, kinew_ref)

    def wait_idx(j, c):
        idx_copy(j).wait()
        return c

    lax.fori_loop(0, n_pages, wait_idx, 0)

    qi2 = qi2_ref[...]
    wcol = wcol_ref[...]
    scale = (H_IDX * D_IDX) ** -0.5

    def head_sum(s):
        r = wcol * jnp.maximum(s, 0.0)
        acc = jnp.zeros((t_new, s.shape[1]), jnp.float32)
        for h in range(H_IDX):
            acc = acc + r[h * t_new:(h + 1) * t_new, :]
        return acc * scale

    pp = 2

    def idx_body(c, carry):
        kk = jnp.concatenate([kidx_buf[c * pp + p] for p in range(pp)], axis=1)
        sc = head_sum(_dot(qi2, kk.astype(MXU_DTYPE)))
        key_ref[:, pl.ds(pl.multiple_of(c * pp * page, pp * page), pp * page)] = _sortable_key(sc)
        return carry

    lax.fori_loop(0, n_pages // pp, idx_body, 0)
    sc_new = head_sum(_dot_nt(qi2, kinew_pad[...]))
    tok = lax.broadcasted_iota(jnp.int32, (t_new, LANES), 0)
    lane = lax.broadcasted_iota(jnp.int32, (t_new, LANES), 1)
    key_ref[:, past:past + kcs] = jnp.full((t_new, kcs), INT_MIN, jnp.int32)
    key_ref[:, past:past + LANES] = jnp.where(lane <= tok, _sortable_key(sc_new), int_min)
    thr = _select_topk(key_ref, (past + kcs) // kcs, kcs, k_sel, kt=min(kcs, 512))

    qblk = qblk_ref[...]

    def head_bias(key_cols):
        bias = jnp.where(key_cols >= thr, 0.0, NEG_INF)
        return jnp.concatenate([bias] * n_heads, axis=0)

    def kv_body(j, carry):
        m, l, acc = carry
        slot = j % 2
        for cpy in kv_copies(j, slot):
            cpy.wait()

        @pl.when(j + 1 < n_chunks)
        def _():
            for cpy in kv_copies(j + 1, 1 - slot):
                cpy.start()

        kt = _chunk_transposed(kbuf, slot)
        vt = _chunk_transposed(vbuf, slot)
        cols = key_ref[:, pl.ds(pl.multiple_of(j * cp * page, cp * page), cp * page)]
        return _softmax_step(_dot(qblk, kt) + head_bias(cols), m, l, acc, vt,
                             v_is_transposed=True)

    rows = n_heads * t_new
    init = (jnp.full((rows, 1), M_INIT, jnp.float32), jnp.zeros((rows, 1), jnp.float32),
            jnp.zeros((rows, dd), jnp.float32))
    m, l, acc = lax.fori_loop(0, n_chunks, kv_body, init)
    s_new = _dot_nt(qblk, knew_pad[...]) + head_bias(key_ref[:, past:past + LANES])
    _, l, acc = _softmax_step(s_new, m, l, acc, vnew_pad[...])
    o_ref[...] = _heads_to_lanes(acc / l, n_heads, t_new)


def _block_diag_queries(qb, b, t, n_heads):
    q4 = qb.reshape(b, t, n_heads, HEAD_DIM)
    eye = jnp.eye(n_heads, dtype=qb.dtype)
    return jnp.einsum("bthd,hg->bhtgd", q4, eye).reshape(b, n_heads * t, n_heads * HEAD_DIM)


def _dsa_sample(qb, kb, vb, qib, kib2, wi, cache_k, cache_v, cache_kidx, page_table, b, t):
    dd = qb.shape[1]
    n_heads = dd // HEAD_DIM
    n_pool, page = cache_k.shape[0], cache_k.shape[1]
    n_pages = page_table.shape[1]
    past = n_pages * page
    k_sel = min(DSA_TOPK_MAX, (past + t) // 4)
    cp = SAMPLE_PAGES_PER_CHUNK
    kcs = min(2048, past)
    assert n_pages % cp == 0 and past % kcs == 0 and t <= LANES and kcs >= k_sel
    assert n_pages % 2 == 0 and t == SUBLANES
    mx = MXU_DTYPE
    qblk = _block_diag_queries(qb, b, t, n_heads)
    qi2 = qib.reshape(b, t, H_IDX, D_IDX).transpose(0, 2, 1, 3).reshape(b, H_IDX * t, D_IDX)
    wcol = wi.reshape(b, t, H_IDX).transpose(0, 2, 1).reshape(b, H_IDX * t, 1)
    kinew = kib2[:, :D_IDX].reshape(b, t, D_IDX)
    per_b = lambda r, n: pl.BlockSpec((None, r, n), lambda bi, pt: (bi, 0, 0))
    anyspec = pl.BlockSpec(memory_space=pl.ANY)
    grid_spec = pltpu.PrefetchScalarGridSpec(
        num_scalar_prefetch=1, grid=(b,),
        in_specs=[per_b(n_heads * t, dd), per_b(H_IDX * t, D_IDX), per_b(H_IDX * t, 1),
                  per_b(t, dd), per_b(t, dd), per_b(t, D_IDX), anyspec, anyspec, anyspec],
        out_specs=per_b(t, dd),
        scratch_shapes=[
            pltpu.VMEM((n_pages, D_IDX, page), jnp.float32),
            pltpu.VMEM((t, past + kcs), jnp.int32),
            pltpu.VMEM((2, cp, n_heads, HEAD_DIM, page), jnp.float32),
            pltpu.VMEM((2, cp, n_heads, HEAD_DIM, page), jnp.float32),
            pltpu.VMEM((LANES, dd), mx), pltpu.VMEM((LANES, dd), mx),
            pltpu.VMEM((LANES, D_IDX), mx),
            pltpu.SemaphoreType.DMA((1,)), pltpu.SemaphoreType.DMA((2,)),
            pltpu.SemaphoreType.DMA((2,)),
        ])
    out = pl.pallas_call(
        functools.partial(_dsa_sample_body, k_sel, n_pages, kcs),
        grid_spec=grid_spec,
        out_shape=jax.ShapeDtypeStruct((b, t, dd), jnp.float32),
        compiler_params=_cparams(("arbitrary",)), name="dsa_sample",
    )(page_table.reshape(-1), qblk, qi2, wcol, kb.reshape(b, t, dd), vb.reshape(b, t, dd),
      kinew, _pages_transposed(cache_kidx), _pages_transposed(cache_k),
      _pages_transposed(cache_v))
    return out.reshape(b * t, dd)


def _moba_sample_body(topk, n_pages, pt_ref, qblk_ref, knew_ref, vnew_ref, ck_ref, cv_ref,
                      o_ref, s_ref, kbuf, vbuf, knew_pad, vnew_pad, sem_k, sem_v):
    b = pl.program_id(0)
    page0 = b * n_pages
    cp = kbuf.shape[1]
    page = kbuf.shape[4]
    dm = kbuf.shape[2] * kbuf.shape[3]
    n_chunks = n_pages // cp
    ck = cp * page
    blk = MOBA_BLOCK
    bpc = ck // blk
    nb = n_pages * page // blk
    t_new = knew_ref.shape[0]
    n_heads = qblk_ref.shape[0] // t_new
    rows = n_heads * t_new

    k_copies = lambda j, slot: _chunk_copies(ck_ref, kbuf, sem_k, pt_ref, page0 + j * cp, slot)
    v_copies = lambda j, slot: _chunk_copies(cv_ref, vbuf, sem_v, pt_ref, page0 + j * cp, slot)
    for cpy in k_copies(0, 0) + v_copies(0, 0):
        cpy.start()
    _pad_rows(knew_pad, knew_ref)
    _pad_rows(vnew_pad, vnew_ref)
    qblk = qblk_ref[...]

    gate_lane = lax.broadcasted_iota(jnp.int32, (rows, LANES), 1)

    def k_body(j, gate):
        slot = j % 2
        for cpy in k_copies(j, slot):
            cpy.wait()

        @pl.when(j + 1 < n_chunks)
        def _():
            for cpy in k_copies(j + 1, 1 - slot):
                cpy.start()

        s = _dot(qblk, _chunk_transposed(kbuf, slot))
        s_ref[:, pl.ds(pl.multiple_of(j * ck, ck), ck)] = s
        for r in range(bpc):
            g = jnp.mean(s[:, r * blk:(r + 1) * blk], axis=1, keepdims=True)
            gate = jnp.where(gate_lane == j * bpc + r, g, gate)
        return gate

    gate = lax.fori_loop(0, n_chunks, k_body, jnp.zeros((rows, LANES), jnp.float32))
    sel = _moba_select(gate, nb, topk).astype(MXU_DTYPE)
    n_row = lax.broadcasted_iota(jnp.int32, (LANES, ck), 0)
    col = lax.broadcasted_iota(jnp.int32, (LANES, ck), 1)
    blk_in_chunk = jnp.zeros((LANES, ck), jnp.int32)
    for r in range(1, bpc):
        blk_in_chunk = blk_in_chunk + jnp.where(col >= r * blk, 1, 0)

    def v_body(j, carry):
        m, l, acc = carry
        slot = j % 2
        for cpy in v_copies(j, slot):
            cpy.wait()

        @pl.when(j + 1 < n_chunks)
        def _():
            for cpy in v_copies(j + 1, 1 - slot):
                cpy.start()

        expand = jnp.where(n_row == j * bpc + blk_in_chunk, 1.0, 0.0).astype(MXU_DTYPE)
        s = jnp.where(_dot(sel, expand) > 0.5,
                      s_ref[:, pl.ds(pl.multiple_of(j * ck, ck), ck)], NEG_INF)
        return _softmax_step(s, m, l, acc, _chunk_transposed(vbuf, slot),
                             v_is_transposed=True)

    init = (jnp.full((rows, 1), M_INIT, jnp.float32), jnp.zeros((rows, 1), jnp.float32),
            jnp.zeros((rows, dm), jnp.float32))
    m, l, acc = lax.fori_loop(0, n_chunks, v_body, init)
    lane = lax.broadcasted_iota(jnp.int32, (rows, LANES), 1)
    s_new = jnp.where(lane <= _row_token(n_heads, t_new), _dot_nt(qblk, knew_pad[...]), NEG_INF)
    _, l, acc = _softmax_step(s_new, m, l, acc, vnew_pad[...])
    o_ref[...] = _heads_to_lanes(acc / l, n_heads, t_new)


def _moba_sample(qb, kb, vb, cache_k, cache_v, page_table, b, t):
    dm = qb.shape[1]
    n_heads = dm // HEAD_DIM
    n_pool, page = cache_k.shape[0], cache_k.shape[1]
    n_pages = page_table.shape[1]
    past = n_pages * page
    cp = SAMPLE_PAGES_PER_CHUNK
    assert past % MOBA_BLOCK == 0 and t <= MOBA_BLOCK and t == SUBLANES
    assert n_pages % cp == 0 and (cp * page) % MOBA_BLOCK == 0
    nb = past // MOBA_BLOCK
    assert nb <= LANES
    topk = min(MOBA_TOPK, nb)
    mx = MXU_DTYPE
    qblk = _block_diag_queries(qb, b, t, n_heads)
    per_b = lambda r, n: pl.BlockSpec((None, r, n), lambda bi, pt: (bi, 0, 0))
    anyspec = pl.BlockSpec(memory_space=pl.ANY)
    grid_spec = pltpu.PrefetchScalarGridSpec(
        num_scalar_prefetch=1, grid=(b,),
        in_specs=[per_b(n_heads * t, dm), per_b(t, dm), per_b(t, dm), anyspec, anyspec],
        out_specs=per_b(t, dm),
        scratch_shapes=[
            pltpu.VMEM((n_heads * t, past), jnp.float32),
            pltpu.VMEM((2, cp, n_heads, HEAD_DIM, page), jnp.float32),
            pltpu.VMEM((2, cp, n_heads, HEAD_DIM, page), jnp.float32),
            pltpu.VMEM((LANES, dm), mx), pltpu.VMEM((LANES, dm), mx),
            pltpu.SemaphoreType.DMA((2,)), pltpu.SemaphoreType.DMA((2,)),
        ])
    out = pl.pallas_call(
        functools.partial(_moba_sample_body, topk, n_pages),
        grid_spec=grid_spec,
        out_shape=jax.ShapeDtypeStruct((b, t, dm), jnp.float32),
        compiler_params=_cparams(("arbitrary",)), name="moba_sample",
    )(page_table.reshape(-1), qblk, kb.reshape(b, t, dm), vb.reshape(b, t, dm),
      _pages_transposed(cache_k), _pages_transposed(cache_v))
    return out.reshape(b * t, dm)


def _trunk(x, p, conv_prev, ffn_prev, dsa_fn, moba_fn, w0_pad, w_conv_a0, wout0_a, wout0_b,
           w_in1, w_out1, ln_mix_g, ln_mix_b, ln_ffn_g, ln_ffn_b, w_ffn_in, w_conv_ffn,
           w_ffn_out, w_ple, w_ple_gate, alpha, c, dd, dm, is_prompt):
    b, t, d = x.shape
    m = b * t
    x2 = x.reshape(m, d)
    p2 = p.reshape(p.shape[0], m, p.shape[-1])

    ya, qb, k0, kb, v0, vb, qib, ki0, kib2, wi, conv_new = _proj0(
        x, w0_pad, w_conv_a0, conv_prev, c, dd)
    yb = dsa_fn(qb, kb, vb, qib, kib2, wi, b, t)
    x2 = _out_ln(x2, [ya, yb], [wout0_a, wout0_b], ln_mix_g[0], ln_mix_b[0], alpha)
    x2, st0 = _ffn(x2, p2[0], b, t, w_ffn_in[0], w_conv_ffn[0], w_ffn_out[0], ln_ffn_g[0],
                   ln_ffn_b[0], w_ple_gate[0], w_ple[0], ffn_prev[0], alpha)

    outs = _proj1(x2.reshape(b, t, d), w_in1, dm, is_prompt)
    qb1, k1, kb1, v1, vb1 = outs[:5]
    yc = moba_fn(qb1, kb1, vb1, *outs[5:], b, t)
    x2 = _out_ln(x2, [yc], [w_out1], ln_mix_g[1], ln_mix_b[1], alpha)
    x2, st1 = _ffn(x2, p2[1], b, t, w_ffn_in[1], w_conv_ffn[1], w_ffn_out[1], ln_ffn_g[1],
                   ln_ffn_b[1], w_ple_gate[1], w_ple[1], ffn_prev[1], alpha)

    h_dsa, h_moba = dd // HEAD_DIM, dm // HEAD_DIM
    return (x2.reshape(b, t, d), k0.reshape(b, t, h_dsa, HEAD_DIM),
            v0.reshape(b, t, h_dsa, HEAD_DIM), ki0.reshape(b, t, D_IDX), conv_new,
            k1.reshape(b, t, h_moba, HEAD_DIM), v1.reshape(b, t, h_moba, HEAD_DIM),
            jnp.stack([st0, st1]))


def kernel(x_prompt, x_sample, cache_k0, cache_v0, cache_kidx0, state_conv0, cache_k1, cache_v1, state_ffn, page_table, p_prompt, p_sample, w_in0, w_conv_a0, w_out0, w_in1, w_out1, ln_mix_g, ln_mix_b, ln_ffn_g, ln_ffn_b, w_ffn_in, w_conv_ffn, w_ffn_out, w_ple, w_ple_gate):
    depth = w_ffn_in.shape[0]
    assert depth == 2
    alpha = (2 * depth) ** 0.25
    d = x_prompt.shape[-1]
    c = w_conv_a0.shape[1]
    dd = cache_k0.shape[2] * HEAD_DIM
    dm = cache_k1.shape[2] * HEAD_DIM
    f = w_conv_ffn.shape[2]
    mx = MXU_DTYPE

    o = 3 * c + 3 * dd + H_IDX * D_IDX
    assert o % LANES == 0 and w_in0.shape[1] == o + D_IDX + H_IDX
    w0 = w_in0.astype(mx)
    ki_cols = w0[:, o:o + D_IDX]
    w0_pad = jnp.concatenate(
        [w0[:, :o], ki_cols, ki_cols, w0[:, o + D_IDX:], jnp.zeros((d, LANES - H_IDX), mx)],
        axis=1)
    wout0 = w_out0.astype(mx)
    shared = (w0_pad, w_conv_a0, wout0[:c], wout0[c:], w_in1.astype(mx), w_out1.astype(mx),
              ln_mix_g, ln_mix_b, ln_ffn_g, ln_ffn_b, w_ffn_in.astype(mx), w_conv_ffn,
              w_ffn_out.astype(mx), w_ple.astype(mx), w_ple_gate.astype(mx), alpha, c, dd, dm)

    bp = x_prompt.shape[0]
    conv_zero = jnp.zeros((bp, CONV_W - 1, c), jnp.float32)
    ffn_zero = jnp.zeros((depth, bp, CONV_W - 1, f), jnp.float32)
    prompt = _trunk(x_prompt, p_prompt, conv_zero, ffn_zero, _dsa_prompt, _moba_prompt,
                    *shared, True)

    def dsa_s(qb, kb, vb, qib, kib2, wi, b, t):
        return _dsa_sample(qb, kb, vb, qib, kib2, wi, cache_k0, cache_v0, cache_kidx0,
                           page_table, b, t)

    def moba_s(qb, kb, vb, b, t):
        return _moba_sample(qb, kb, vb, cache_k1, cache_v1, page_table, b, t)

    sample = _trunk(x_sample, p_sample, state_conv0, state_ffn, dsa_s, moba_s, *shared, False)
    y_p, k0_p, v0_p, ki_p, conv_p, k1_p, v1_p, ffn_p = prompt
    y_s, k0_s, v0_s, ki_s, conv_s, k1_s, v1_s, ffn_s = sample
    return (y_p, y_s, k0_p, v0_p, ki_p, conv_p, k1_p, v1_p, ffn_p,
            k0_s, v0_s, ki_s, conv_s, k1_s, v1_s, ffn_s)
```

```python
import functools

import jax
import jax.numpy as jnp
from jax import lax
from jax.experimental import pallas as pl
from jax.experimental.pallas import tpu as pltpu

HEAD_DIM = 64
H_IDX = 8
D_IDX = 64
DSA_TOPK_MAX = 256
MOBA_BLOCK = 256
MOBA_TOPK = 3
CONV_W = 3
LN_EPS = 1e-5

LANES = 128
SUBLANES = 8
VMEM_LIMIT_BYTES = 56 * 1024 * 1024
ROW_TILE = 256
DSA_TQ = 256
DSA_KC = 512
DSA_KA = 1024
MOBA_KC = 1024
SAMPLE_PAGES_PER_CHUNK = 4
SAMPLE_RING_SLOTS = 4
SAMPLE_SELECT_KC = 2048
INDEXER_PAGES_PER_STEP = 2

MXU_DTYPE = jnp.bfloat16
SELECT_ROW_BLOCK = 128
INT_MIN = -(2 ** 31)
NEG_INF = float("-inf")
M_INIT = -(2.0 ** 100)
MASK_BIAS = -(2.0 ** 101)


def _cparams(semantics):
    return pltpu.CompilerParams(dimension_semantics=semantics,
                                vmem_limit_bytes=VMEM_LIMIT_BYTES)


def _resident(shape):
    nd = len(shape)
    return pl.BlockSpec(shape, lambda *_: (0,) * nd, pipeline_mode=pl.Buffered(1))


def _dot(a, b):
    return jnp.dot(a, b, preferred_element_type=jnp.float32)


def _dot_nt(a, b):
    return lax.dot_general(a, b, (((1,), (1,)), ((), ())),
                           preferred_element_type=jnp.float32)


def _layer_norm(z, g, b):
    mu = jnp.mean(z, axis=-1, keepdims=True)
    var = jnp.mean(jnp.square(z - mu), axis=-1, keepdims=True)
    return (z - mu) * lax.rsqrt(var + LN_EPS) * g + b


def _conv3_carry(u, w_ref, ubuf_ref, prev_ref, first_tile):
    tm = u.shape[0]

    @pl.when(first_tile)
    def _():
        ubuf_ref[SUBLANES - 2:SUBLANES, :] = prev_ref[0]

    ubuf_ref[SUBLANES:SUBLANES + tm, :] = u
    u1 = ubuf_ref[SUBLANES - 1:SUBLANES - 1 + tm, :]
    u2 = ubuf_ref[SUBLANES - 2:SUBLANES - 2 + tm, :]
    y = u2 * w_ref[0:1, :] + u1 * w_ref[1:2, :] + u * w_ref[2:3, :]
    ubuf_ref[0:SUBLANES, :] = ubuf_ref[tm:tm + SUBLANES, :]
    return y


def _conv3_rows(u, w_ref, ubuf_ref, p1_ref, p2_ref, tpos_ref):
    tm = u.shape[0]
    ubuf_ref[0:SUBLANES, :] = jnp.zeros((SUBLANES, u.shape[1]), jnp.float32)
    ubuf_ref[SUBLANES:SUBLANES + tm, :] = u
    t = tpos_ref[...]
    u1 = jnp.where(t >= 1, ubuf_ref[SUBLANES - 1:SUBLANES - 1 + tm, :], p1_ref[...])
    u2 = jnp.where(t >= 2, ubuf_ref[SUBLANES - 2:SUBLANES - 2 + tm, :], p2_ref[...])
    return u2 * w_ref[0:1, :] + u1 * w_ref[1:2, :] + u * w_ref[2:3, :]


def _conv_prev_rows(prev, t_len):
    b, _, c = prev.shape
    z = jnp.zeros((b, t_len, c), jnp.float32)
    p1 = z.at[:, 0].set(prev[:, 1])
    p2 = z.at[:, 0].set(prev[:, 0]).at[:, 1].set(prev[:, 1])
    return p1.reshape(b * t_len, c), p2.reshape(b * t_len, c)


def _tpos(b, t_len):
    return jnp.tile(jnp.arange(t_len, dtype=jnp.int32), b).reshape(b * t_len, 1)


def _proj0_body(carry_mode, c, dd, x_ref, w_ref, wc_ref, *rest):
    if carry_mode:
        (prev_ref, ya_ref, qb_ref, k_ref, kb_ref, v_ref, vb_ref, qib_ref, ki_ref,
         kib2_ref, wi_ref, st_ref, ubuf_ref) = rest
    else:
        (p1_ref, p2_ref, tpos_ref, ya_ref, qb_ref, k_ref, kb_ref, v_ref, vb_ref,
         qib_ref, ki_ref, kib2_ref, wi_ref, u_ref, ubuf_ref) = rest
    res = _dot(x_ref[...].astype(MXU_DTYPE), w_ref[...])
    bg = res[:, 0:c]
    u = res[:, c:2 * c] * res[:, 2 * c:3 * c]
    if carry_mode:
        y = _conv3_carry(u, wc_ref, ubuf_ref, prev_ref, pl.program_id(1) == 0)
        st_ref[0] = u[u.shape[0] - 2:, :]
    else:
        y = _conv3_rows(u, wc_ref, ubuf_ref, p1_ref, p2_ref, tpos_ref)
        u_ref[...] = u
    ya_ref[...] = bg * y
    o = 3 * c
    q = res[:, o:o + dd]
    k = res[:, o + dd:o + 2 * dd]
    v = res[:, o + 2 * dd:o + 3 * dd]
    qb_ref[...] = (q * HEAD_DIM ** -0.5).astype(MXU_DTYPE)
    k_ref[...] = k
    kb_ref[...] = k.astype(MXU_DTYPE)
    v_ref[...] = v
    vb_ref[...] = v.astype(MXU_DTYPE)
    o += 3 * dd
    di = H_IDX * D_IDX
    qib_ref[...] = res[:, o:o + di].astype(MXU_DTYPE)
    o += di
    ki_ref[...] = res[:, o:o + D_IDX]
    kib2_ref[...] = res[:, o:o + 2 * D_IDX].astype(MXU_DTYPE)
    o += 2 * D_IDX
    wi_ref[...] = res[:, o:o + H_IDX]


def _proj0(x, w_pad, w_conv, prev, c, dd):
    b, t, d = x.shape
    m = b * t
    npad = w_pad.shape[1]
    di = H_IDX * D_IDX
    carry_mode = t % ROW_TILE == 0
    x2 = x.reshape(m, d)
    f32, mx = jnp.float32, MXU_DTYPE
    row_shapes = [(c, f32), (dd, mx), (dd, f32), (dd, mx), (dd, f32), (dd, mx),
                  (di, mx), (D_IDX, f32), (2 * D_IDX, mx), (H_IDX, f32)]
    if carry_mode:
        tm = ROW_TILE
        nt = t // tm
        grid = (b, nt)
        row = lambda n: pl.BlockSpec((tm, n), lambda bi, j: (bi * nt + j, 0))
        in_specs = [row(d), _resident((d, npad)), _resident((CONV_W, c)),
                    pl.BlockSpec((1, CONV_W - 1, c), lambda bi, j: (bi, 0, 0))]
        out_specs = [row(n) for n, _ in row_shapes] + [
            pl.BlockSpec((1, CONV_W - 1, c), lambda bi, j: (bi, 0, 0))]
        out_shape = [jax.ShapeDtypeStruct((m, n), dt) for n, dt in row_shapes] + [
            jax.ShapeDtypeStruct((b, CONV_W - 1, c), f32)]
        args = (x2, w_pad, w_conv, prev)
        sem = ("arbitrary", "arbitrary")
    else:
        assert m % SUBLANES == 0 and t >= CONV_W - 1
        tm = m
        grid = (1,)
        row = lambda n: pl.BlockSpec((tm, n), lambda i: (0, 0))
        p1, p2 = _conv_prev_rows(prev, t)
        in_specs = [row(d), _resident((d, npad)), _resident((CONV_W, c)),
                    row(c), row(c), row(1)]
        out_specs = [row(n) for n, _ in row_shapes] + [row(c)]
        out_shape = [jax.ShapeDtypeStruct((m, n), dt) for n, dt in row_shapes] + [
            jax.ShapeDtypeStruct((m, c), f32)]
        args = (x2, w_pad, w_conv, p1, p2, _tpos(b, t))
        sem = ("arbitrary",)
    outs = pl.pallas_call(
        functools.partial(_proj0_body, carry_mode, c, dd),
        grid=grid, in_specs=in_specs, out_specs=out_specs, out_shape=out_shape,
        scratch_shapes=[pltpu.VMEM((tm + 2 * SUBLANES, c), f32)],
        compiler_params=_cparams(sem), name="proj0",
    )(*args)
    outs = list(outs)
    if not carry_mode:
        outs[-1] = outs[-1].reshape(b, t, c)[:, t - (CONV_W - 1):, :]
    return outs


def _proj1_body(blocks_per_seq, dm, x_ref, w_ref, qb_ref, k_ref, ka_ref, v_ref, va_ref, *rest):
    res = _dot(x_ref[...].astype(MXU_DTYPE), w_ref[...])
    k = res[:, dm:2 * dm]
    v = res[:, 2 * dm:3 * dm]
    qb_ref[...] = (res[:, 0:dm] * HEAD_DIM ** -0.5).astype(MXU_DTYPE)
    k_ref[...] = k
    v_ref[...] = v
    if blocks_per_seq is None:
        ka_ref[...] = k.astype(MXU_DTYPE)
        va_ref[...] = v.astype(MXU_DTYPE)
        return
    rest[0][0] = jnp.mean(k, axis=0, keepdims=True)
    blk = pl.program_id(0) % blocks_per_seq
    lane = lax.broadcasted_iota(jnp.int32, (k.shape[0], LANES), 1)
    low = lane < HEAD_DIM
    e_high = jnp.where(lane == HEAD_DIM + blk, MASK_BIAS, 0.0)
    e_low = jnp.where(lane == blk, MASK_BIAS, 0.0)
    for hp in range(dm // LANES):
        kp = k[:, hp * LANES:(hp + 1) * LANES]
        vp = v[:, hp * LANES:(hp + 1) * LANES]
        even = slice(2 * hp * LANES, (2 * hp + 1) * LANES)
        odd = slice((2 * hp + 1) * LANES, (2 * hp + 2) * LANES)
        ka_ref[:, even] = jnp.where(low, kp, e_high).astype(MXU_DTYPE)
        ka_ref[:, odd] = jnp.where(low, e_low, kp).astype(MXU_DTYPE)
        va_ref[:, even] = jnp.where(low, vp, 1.0).astype(MXU_DTYPE)
        va_ref[:, odd] = jnp.where(low, 1.0, vp).astype(MXU_DTYPE)


def _proj1(x, w, dm, is_prompt):
    b, t, d = x.shape
    m = b * t
    f32, mx = jnp.float32, MXU_DTYPE
    if is_prompt:
        assert t % MOBA_BLOCK == 0 and t // MOBA_BLOCK <= HEAD_DIM
        tm = MOBA_BLOCK
        da = 2 * dm
        blocks_per_seq = t // MOBA_BLOCK
    else:
        tm = m
        da = dm
        blocks_per_seq = None
    nt = m // tm
    row = lambda n: pl.BlockSpec((tm, n), lambda i: (i, 0))
    out_specs = [row(dm), row(dm), row(da), row(dm), row(da)]
    out_shape = [jax.ShapeDtypeStruct((m, n), dt)
                 for n, dt in ((dm, mx), (dm, f32), (da, mx), (dm, f32), (da, mx))]
    if is_prompt:
        out_specs.append(pl.BlockSpec((1, 1, dm), lambda i: (i, 0, 0)))
        out_shape.append(jax.ShapeDtypeStruct((nt, 1, dm), f32))
    return pl.pallas_call(
        functools.partial(_proj1_body, blocks_per_seq, dm),
        grid=(nt,), in_specs=[row(d), _resident((d, 3 * dm))],
        out_specs=out_specs, out_shape=out_shape,
        compiler_params=_cparams(("arbitrary",)), name="proj1",
    )(x.reshape(m, d), w)


def _out_ln_body(n_act, alpha, x_ref, *rest):
    acts = rest[:n_act]
    ws = rest[n_act:2 * n_act]
    g_ref, b_ref, o_ref = rest[2 * n_act:]
    y = _dot(acts[0][...].astype(MXU_DTYPE), ws[0][...])
    for a_ref, w_ref in zip(acts[1:], ws[1:]):
        y = y + _dot(a_ref[...].astype(MXU_DTYPE), w_ref[...])
    o_ref[...] = _layer_norm(alpha * x_ref[...] + y, g_ref[...], b_ref[...])


def _out_ln(x2, acts, ws, g, bta, alpha):
    m, d = x2.shape
    tm = ROW_TILE if m % ROW_TILE == 0 else m
    row = lambda n: pl.BlockSpec((tm, n), lambda i: (i, 0))
    in_specs = ([row(d)] + [row(a.shape[1]) for a in acts]
                + [_resident(w.shape) for w in ws] + [_resident((1, d))] * 2)
    return pl.pallas_call(
        functools.partial(_out_ln_body, len(acts), alpha),
        grid=(m // tm,), in_specs=in_specs, out_specs=row(d),
        out_shape=jax.ShapeDtypeStruct((m, d), jnp.float32),
        compiler_params=_cparams(("arbitrary",)), name="out_ln",
    )(x2, *acts, *ws, g.reshape(1, d), bta.reshape(1, d))


def _ffn_body(carry_mode, f, alpha, x_ref, p_ref, win_ref, wc_ref, wout_ref, g_ref, b_ref,
              wg_ref, wp_ref, *rest):
    if carry_mode:
        prev_ref, o_ref, st_ref, ubuf_ref = rest
    else:
        p1_ref, p2_ref, tpos_ref, o_ref, u_ref, ubuf_ref = rest
    x = x_ref[...]
    h = _dot(x.astype(MXU_DTYPE), win_ref[...])
    u = h[:, 0:f]
    if carry_mode:
        uc = _conv3_carry(u, wc_ref, ubuf_ref, prev_ref, pl.program_id(1) == 0)
        st_ref[0] = u[u.shape[0] - 2:, :]
    else:
        uc = _conv3_rows(u, wc_ref, ubuf_ref, p1_ref, p2_ref, tpos_ref)
        u_ref[...] = u
    a = jax.nn.gelu(uc) * h[:, f:2 * f]
    y = _dot(a.astype(MXU_DTYPE), wout_ref[...])
    x2 = _layer_norm(alpha * x + y, g_ref[...], b_ref[...])
    gate = jax.nn.sigmoid(_dot(x2.astype(MXU_DTYPE), wg_ref[...]))
    o_ref[...] = x2 + gate * _dot(p_ref[...].astype(MXU_DTYPE), wp_ref[...])


def _ffn(x2, p2d, b, t, win, wc, wout, g, bta, wg, wp, prev, alpha):
    m, d = x2.shape
    f = wc.shape[1]
    dp = p2d.shape[1]
    carry_mode = t % ROW_TILE == 0
    f32 = jnp.float32
    weights = [_resident((d, 2 * f)), _resident((CONV_W, f)), _resident((f, d)),
               _resident((1, d)), _resident((1, d)), _resident((d, d)), _resident((dp, d))]
    if carry_mode:
        tm = ROW_TILE
        nt = t // tm
        grid = (b, nt)
        row = lambda n: pl.BlockSpec((tm, n), lambda bi, j: (bi * nt + j, 0))
        st_spec = pl.BlockSpec((1, CONV_W - 1, f), lambda bi, j: (bi, 0, 0))
        in_specs = [row(d), row(dp)] + weights + [st_spec]
        out_specs = [row(d), st_spec]
        out_shape = [jax.ShapeDtypeStruct((m, d), f32),
                     jax.ShapeDtypeStruct((b, CONV_W - 1, f), f32)]
        extra = (prev,)
        sem = ("arbitrary", "arbitrary")
    else:
        tm = m
        grid = (1,)
        row = lambda n: pl.BlockSpec((tm, n), lambda i: (0, 0))
        p1, p2 = _conv_prev_rows(prev, t)
        in_specs = [row(d), row(dp)] + weights + [row(f), row(f), row(1)]
        out_specs = [row(d), row(f)]
        out_shape = [jax.ShapeDtypeStruct((m, d), f32), jax.ShapeDtypeStruct((m, f), f32)]
        extra = (p1, p2, _tpos(b, t))
        sem = ("arbitrary",)
    xo, st = pl.pallas_call(
        functools.partial(_ffn_body, carry_mode, f, alpha),
        grid=grid, in_specs=in_specs, out_specs=out_specs, out_shape=out_shape,
        scratch_shapes=[pltpu.VMEM((tm + 2 * SUBLANES, f), f32)],
        compiler_params=_cparams(sem), name="ffn",
    )(x2, p2d, win, wc, wout, g.reshape(1, d), bta.reshape(1, d), wg, wp, *extra)
    if not carry_mode:
        st = st.reshape(b, t, f)[:, t - (CONV_W - 1):, :]
    return xo, st


def _sortable_key(sc):
    bits = pltpu.bitcast(sc, jnp.int32)
    return bits ^ ((bits >> 31) & 0x7FFFFFFF)


def _fold_lanes(x):
    part = x[:, 0:LANES]
    for j in range(1, x.shape[1] // LANES):
        part = part + x[:, j * LANES:(j + 1) * LANES]
    return part


def _row_blocks(rows):
    rb = min(rows, SELECT_ROW_BLOCK)
    assert rows % rb == 0
    return [(r0, rb) for r0 in range(0, rows, rb)]


def _stack_rows(parts):
    return parts[0] if len(parts) == 1 else jnp.concatenate(parts, axis=0)


def _count_ge(key_ref, cand, n_chunks, kc):
    outs = []
    for r0, rb in _row_blocks(key_ref.shape[0]):
        cr = jnp.broadcast_to(cand[r0:r0 + rb, :], (rb, LANES))

        def body(c, acc, r0=r0, rb=rb, cr=cr):
            kk = key_ref[r0:r0 + rb, pl.ds(pl.multiple_of(c * kc, kc), kc)]
            for j in range(kc // LANES):
                acc = acc + jnp.where(kk[:, j * LANES:(j + 1) * LANES] >= cr, 1.0, 0.0)
            return acc

        acc = lax.fori_loop(0, n_chunks, body, jnp.zeros((rb, LANES), jnp.float32))
        outs.append(jnp.sum(acc, axis=1, keepdims=True))
    return _stack_rows(outs)


def _select_topk(key_ref, n_chunks, kc, k_sel, kt=None):
    rows = key_ref.shape[0]
    int_min = jnp.int32(INT_MIN)
    kt = kc if kt is None else kt
    assert kc % kt == 0

    def bit_body(i, t_u):
        cand_u = t_u | lax.shift_left(jnp.int32(1), 31 - i)
        cnt = _count_ge(key_ref, cand_u ^ int_min, n_chunks, kc)
        return jnp.where(cnt >= k_sel, cand_u, t_u)

    t_u = lax.fori_loop(0, 32, bit_body, jnp.zeros((rows, 1), jnp.int32))
    thr = jnp.maximum(t_u ^ int_min, int_min + 1)
    cnt_ge = _count_ge(key_ref, thr, n_chunks, kc)
    excess = jnp.max(jnp.where(cnt_ge > k_sel, 1.0, 0.0))

    @pl.when(excess > 0.0)
    def _():
        cnt_gt = _count_ge(key_ref, thr + 1, n_chunks, kc)
        need = k_sel - cnt_gt
        r_i = lax.broadcasted_iota(jnp.int32, (kt, kt), 0)
        c_i = lax.broadcasted_iota(jnp.int32, (kt, kt), 1)
        tri = jnp.where(r_i <= c_i, 1.0, 0.0).astype(MXU_DTYPE)

        def body(c, seen):
            sl = pl.ds(pl.multiple_of(c * kt, kt), kt)
            kk = key_ref[:, sl]
            tie = kk == thr
            tie_f = jnp.where(tie, 1.0, 0.0)
            rank = _dot(tie_f.astype(MXU_DTYPE), tri) + seen
            key_ref[:, sl] = jnp.where(tie & (rank > need), int_min, kk)
            return seen + jnp.sum(tie_f, axis=1, keepdims=True)

        lax.fori_loop(0, n_chunks * (kc // kt), body, jnp.zeros((rows, 1), jnp.float32))

    return thr


def _softmax_step(s, m, l, acc, v, v_is_transposed=False):
    m_new = jnp.maximum(m, jnp.max(s, axis=1, keepdims=True))
    a = jnp.exp(m - m_new)
    p = jnp.exp(s - m_new)
    l_new = a * l + jnp.sum(p, axis=1, keepdims=True)
    pv = (_dot_nt if v_is_transposed else _dot)(p.astype(MXU_DTYPE), v)
    return m_new, l_new, a * acc + pv


def _head_halves(x):
    lane = lax.broadcasted_iota(jnp.int32, x.shape, 1)
    zero = jnp.zeros_like(x)
    return jnp.where(lane < HEAD_DIM, x, zero), jnp.where(lane >= HEAD_DIM, x, zero)


def _dsa_prompt_body(k_sel, qb_ref, kb_ref, vb_ref, qib_ref, kib2_ref, wi_ref, o_ref,
                     key_ref, thr_ref, m_ref, l_ref, acc_ref, sa_ref, sb_ref):
    i = pl.program_id(1)
    hp = pl.program_id(2)
    tq = qb_ref.shape[0]
    kc = DSA_KC
    ka = min(DSA_KA, kb_ref.shape[0])
    n_att = ((i + 1) * tq + ka - 1) // ka
    n_chunks = n_att * (ka // kc)
    qpos = i * tq + lax.broadcasted_iota(jnp.int32, (tq, 1), 0)

    @pl.when(hp == 0)
    def _():
        qi = qib_ref[...]
        w = wi_ref[...]
        qm = []
        for hh in range(H_IDX // 2):
            qm.extend(_head_halves(qi[:, hh * LANES:(hh + 1) * LANES]))
        qstack = jnp.concatenate(qm, axis=0)
        wcol = [w[:, h:h + 1] for h in range(H_IDX)]

        def idx_body(c, carry):
            sl = pl.ds(pl.multiple_of(c * kc, kc), kc)
            s_all = _dot_nt(qstack, kib2_ref[sl, :])
            acc = jnp.zeros((tq, kc), jnp.float32)
            for h in range(H_IDX):
                acc = acc + wcol[h] * jnp.maximum(s_all[h * tq:(h + 1) * tq, :], 0.0)
            sc = acc * (H_IDX * D_IDX) ** -0.5
            kpos = c * kc + lax.broadcasted_iota(jnp.int32, (1, kc), 1)
            key_ref[:, sl] = jnp.where(kpos <= qpos, _sortable_key(sc), jnp.int32(INT_MIN))
            return carry

        lax.fori_loop(0, n_chunks, idx_body, 0)
        thr_ref[...] = _select_topk(key_ref, n_chunks, kc, k_sel)

    thr = thr_ref[...]
    qs = _head_halves(qb_ref[...])
    m_ref[...] = jnp.full(m_ref.shape, M_INIT, jnp.float32)
    l_ref[...] = jnp.zeros(l_ref.shape, jnp.float32)
    acc_ref[...] = jnp.zeros(acc_ref.shape, jnp.float32)

    def scores_into(s_ref, c):
        rows = pl.ds(pl.multiple_of(c * ka, ka), ka)
        kp = kb_ref[rows, :]
        sel = key_ref[:, rows] >= thr
        for h in range(2):
            s_ref[h] = jnp.where(sel, _dot_nt(qs[h], kp), MASK_BIAS)

    def consume(s_ref, c):
        vp = vb_ref[pl.ds(pl.multiple_of(c * ka, ka), ka), :]
        for h in range(2):
            s = s_ref[h]
            m_old = m_ref[h]
            m_new = jnp.maximum(m_old, jnp.max(s, axis=1, keepdims=True))
            a = jnp.exp(m_old - m_new)
            p = jnp.exp(s - m_new)
            l_ref[h] = a * l_ref[h] + jnp.sum(p, axis=1, keepdims=True)
            acc_ref[h] = a * acc_ref[h] + _dot(p.astype(MXU_DTYPE), vp)
            m_ref[h] = m_new

    _two_stage_loop(n_att, scores_into, consume, sa_ref, sb_ref)
    lane = lax.broadcasted_iota(jnp.int32, (tq, LANES), 1)
    o_ref[...] = jnp.where(lane < HEAD_DIM, acc_ref[0] / l_ref[0], acc_ref[1] / l_ref[1])


def _dsa_prompt(qb, kb, vb, qib, kib2, wi, b, t):
    dd = qb.shape[1]
    k_sel = min(DSA_TOPK_MAX, t // 4)
    tq, kc, ka = DSA_TQ, DSA_KC, min(DSA_KA, t)
    assert t % ka == 0 and ka % kc == 0 and ka % tq == 0 and kc >= k_sel and dd % LANES == 0
    nq, n_hp = t // tq, dd // LANES
    r3 = lambda a: a.reshape(b, t, a.shape[1])
    di = qib.shape[1]
    f32 = jnp.float32
    out = pl.pallas_call(
        functools.partial(_dsa_prompt_body, k_sel),
        grid=(b, nq, n_hp),
        in_specs=[
            pl.BlockSpec((None, tq, LANES), lambda bi, i, hp: (bi, i, hp)),
            pl.BlockSpec((None, t, LANES), lambda bi, i, hp: (bi, 0, hp)),
            pl.BlockSpec((None, t, LANES), lambda bi, i, hp: (bi, 0, hp)),
            pl.BlockSpec((None, tq, di), lambda bi, i, hp: (bi, i, 0)),
            pl.BlockSpec((None, t, 2 * D_IDX), lambda bi, i, hp: (bi, 0, 0),
                         pipeline_mode=pl.Buffered(1)),
            pl.BlockSpec((None, tq, H_IDX), lambda bi, i, hp: (bi, i, 0)),
        ],
        out_specs=pl.BlockSpec((None, tq, LANES), lambda bi, i, hp: (bi, i, hp)),
        out_shape=jax.ShapeDtypeStruct((b, t, dd), f32),
        scratch_shapes=[pltpu.VMEM((tq, t), jnp.int32), pltpu.VMEM((tq, 1), jnp.int32),
                        pltpu.VMEM((2, tq, 1), f32), pltpu.VMEM((2, tq, 1), f32),
                        pltpu.VMEM((2, tq, LANES), f32),
                        pltpu.VMEM((2, tq, ka), f32), pltpu.VMEM((2, tq, ka), f32)],
        compiler_params=_cparams(("arbitrary", "arbitrary", "arbitrary")),
        name="dsa_prompt",
    )(r3(qb), r3(kb), r3(vb), r3(qib), r3(kib2), r3(wi))
    return out.reshape(b * t, dd)


def _moba_select(gate, first_lane, n_valid, topk):
    rows, n = gate.shape
    n_idx = lax.broadcasted_iota(jnp.int32, (rows, n), 1)

    def in_window(x, other):
        return jnp.where(n_idx >= first_lane,
                         jnp.where(n_idx < first_lane + n_valid, x, other), other)

    g = in_window(gate, NEG_INF)
    picked = jnp.zeros((rows, n), jnp.float32)
    for _ in range(topk):
        mx = jnp.max(g, axis=1, keepdims=True)
        first = jnp.min(jnp.where(g == mx, n_idx, n), axis=1, keepdims=True)
        hit = n_idx == first
        picked = jnp.where(hit, 1.0, picked)
        g = jnp.where(hit, NEG_INF, g)
    return in_window(picked, 0.0)


def _two_stage_loop(n, produce, consume, buf_a, buf_b):
    last = jnp.maximum(n - 1, 0)

    @pl.when(n > 0)
    def _():
        produce(buf_a, 0)

    def pair(p, carry):
        c0 = 2 * p
        produce(buf_b, jnp.minimum(c0 + 1, last))
        consume(buf_a, c0)

        @pl.when(c0 + 1 < n)
        def _():
            produce(buf_a, jnp.minimum(c0 + 2, last))
            consume(buf_b, c0 + 1)

        return carry

    lax.fori_loop(0, (n + 1) // 2, pair, 0)


def _moba_prompt_body(topk, qb_ref, ka_ref, va_ref, kmx_ref, o_ref, m_ref, acc_ref,
                      sa_ref, sb_ref):
    i = pl.program_id(2)
    tq = qb_ref.shape[0]
    blk = MOBA_BLOCK
    kc = min(MOBA_KC, ka_ref.shape[0])
    q2 = qb_ref[...]
    lane = lax.broadcasted_iota(jnp.int32, (tq, LANES), 1)
    low = lane < HEAD_DIM
    q0, q1 = _head_halves(q2)
    kmx = kmx_ref[...].astype(MXU_DTYPE)
    pick0 = _moba_select(_dot_nt(q0, kmx[0:LANES]), HEAD_DIM, i, topk)
    pick1 = _moba_select(_dot_nt(q1, kmx[LANES:2 * LANES]), 0, i, topk)
    zero = jnp.zeros_like(q2)
    qa = (jnp.where(low, q2, (1.0 - pick0).astype(MXU_DTYPE)),
          jnp.where(low, (1.0 - pick1).astype(MXU_DTYPE), q2))
    m_ref[...] = jnp.full(m_ref.shape, M_INIT, jnp.float32)
    acc_ref[...] = jnp.zeros(acc_ref.shape, jnp.float32)

    def update(h, s, v):
        m_old = m_ref[h]
        m_new = jnp.maximum(m_old, jnp.max(s, axis=1, keepdims=True))
        p = jnp.exp(s - m_new).astype(MXU_DTYPE)
        acc_ref[h] = jnp.exp(m_old - m_new) * acc_ref[h] + _dot(p, v)
        m_ref[h] = m_new

    def scores_into(s_ref, c):
        rows = pl.ds(pl.multiple_of(c * kc, kc), kc)
        for h in range(2):
            s_ref[h] = _dot_nt(qa[h], ka_ref[rows, h * LANES:(h + 1) * LANES])

    def consume(s_ref, c):
        rows = pl.ds(pl.multiple_of(c * kc, kc), kc)
        for h in range(2):
            update(h, s_ref[h], va_ref[rows, h * LANES:(h + 1) * LANES])

    _two_stage_loop((i * blk + kc - 1) // kc, scores_into, consume, sa_ref, sb_ref)

    rows = pl.ds(pl.multiple_of(i * blk, blk), blk)
    causal = (lax.broadcasted_iota(jnp.int32, (tq, blk), 1)
              <= lax.broadcasted_iota(jnp.int32, (tq, blk), 0))
    own_lane = (HEAD_DIM + i, i)
    for h in range(2):
        cols = slice(h * LANES, (h + 1) * LANES)
        q_own = jnp.where(lane == own_lane[h], zero, qa[h])
        s = jnp.where(causal, _dot_nt(q_own, ka_ref[rows, cols]), MASK_BIAS)
        update(h, s, va_ref[rows, cols])
    a0, a1 = acc_ref[0], acc_ref[1]
    o_ref[...] = jnp.where(low, a0 / pltpu.roll(a0, HEAD_DIM, 1), a1 / pltpu.roll(a1, HEAD_DIM, 1))


def _moba_prompt(qb, ka, va, kmean, b, t):
    dm = qb.shape[1]
    blk = MOBA_BLOCK
    nb = t // blk
    n_hp = dm // LANES
    assert t % blk == 0 and dm % LANES == 0 and nb <= HEAD_DIM and t % min(MOBA_KC, t) == 0
    topk = min(MOBA_TOPK, nb - 1)
    kmp = kmean.reshape(b, nb, n_hp, LANES).transpose(0, 2, 1, 3)
    kmp = jnp.pad(kmp, ((0, 0), (0, 0), (0, HEAD_DIM - nb), (0, 0)))
    z = jnp.zeros_like(kmp)
    kmx = jnp.concatenate([z, kmp, kmp, z], axis=2)
    resident = lambda n: pl.BlockSpec((None, t, n), lambda bi, hp, i: (bi, 0, hp),
                                      pipeline_mode=pl.Buffered(1))
    out = pl.pallas_call(
        functools.partial(_moba_prompt_body, topk),
        grid=(b, n_hp, nb),
        in_specs=[
            pl.BlockSpec((None, blk, LANES), lambda bi, hp, i: (bi, i, hp)),
            resident(2 * LANES), resident(2 * LANES),
            pl.BlockSpec((None, None, 2 * LANES, LANES), lambda bi, hp, i: (bi, hp, 0, 0)),
        ],
        out_specs=pl.BlockSpec((None, blk, LANES), lambda bi, hp, i: (bi, i, hp)),
        out_shape=jax.ShapeDtypeStruct((b, t, dm), jnp.float32),
        scratch_shapes=[pltpu.VMEM((2, blk, 1), jnp.float32),
                        pltpu.VMEM((2, blk, LANES), jnp.float32),
                        pltpu.VMEM((2, blk, min(MOBA_KC, t)), jnp.float32),
                        pltpu.VMEM((2, blk, min(MOBA_KC, t)), jnp.float32)],
        compiler_params=_cparams(("arbitrary", "arbitrary", "arbitrary")),
        name="moba_prompt",
    )(qb.reshape(b, t, dm), ka.reshape(b, t, 2 * dm), va.reshape(b, t, 2 * dm), kmx)
    return out.reshape(b * t, dm)


def _chunk_copies(cache_ref, buf_ref, sem, pt_ref, g):
    n_slots, cp = buf_ref.shape[0], buf_ref.shape[1]
    slot = g % n_slots
    return [pltpu.make_async_copy(cache_ref.at[pt_ref[g * cp + p]],
                                  buf_ref.at[slot, p], sem.at[slot])
            for p in range(cp)]


def _stream_prologue(copies, n_total, lookahead):
    assert n_total >= lookahead

    @pl.when(pl.program_id(0) == 0)
    def _():
        for g in range(lookahead):
            for cpy in copies(g):
                cpy.start()


def _stream_wait(copies, g, n_total, lookahead):
    for cpy in copies(g):
        cpy.wait()

    @pl.when(g + lookahead < n_total)
    def _():
        for cpy in copies(g + lookahead):
            cpy.start()


def _chunk_transposed(buf_ref, slot):
    cp, nh, dh, page = buf_ref.shape[1:]
    return jnp.concatenate([buf_ref[slot, p].reshape(nh * dh, page) for p in range(cp)],
                           axis=1).astype(MXU_DTYPE)


def _pages_transposed(cache):
    nd = cache.ndim
    return jnp.transpose(cache, (0,) + tuple(range(2, nd)) + (1,))


def _pad_rows(dst_ref, src_ref):
    dst_ref[...] = jnp.zeros(dst_ref.shape, dst_ref.dtype)
    dst_ref[0:src_ref.shape[0], :] = src_ref[...]


def _row_token(n_heads, t_new):
    return lax.broadcasted_iota(jnp.int32, (n_heads, t_new, LANES), 1).reshape(
        n_heads * t_new, LANES)


def _heads_to_lanes(x, n_heads, t_new):
    lane_head = lax.broadcasted_iota(jnp.int32, (t_new, x.shape[1]), 1) // HEAD_DIM
    out = jnp.zeros((t_new, x.shape[1]), jnp.float32)
    for h in range(n_heads):
        out = jnp.where(lane_head == h, x[h * t_new:(h + 1) * t_new, :], out)
    return out


def _dsa_sample_body(k_sel, n_seq, n_pages, kcs, pt_ref, qblk_ref, qi2_ref, wcol_ref,
                     knew_ref, vnew_ref, kinew_ref, ckidx_ref, ck_ref, cv_ref, o_ref,
                     kidx_buf, key_ref, kbuf, vbuf, knew_pad, vnew_pad, kinew_pad,
                     sem_i, sem_k, sem_v):
    b = pl.program_id(0)
    n_slots, cp = kbuf.shape[0], kbuf.shape[1]
    lookahead = n_slots - 1
    page = kbuf.shape[4]
    dd = kbuf.shape[2] * kbuf.shape[3]
    n_chunks = n_pages // cp
    n_total = n_seq * n_chunks
    past = n_pages * page
    t_new = knew_ref.shape[0]
    n_heads = qblk_ref.shape[0] // t_new
    int_min = jnp.int32(INT_MIN)

    def idx_pages(seq, action):
        def body(j, c):
            cpy = pltpu.make_async_copy(ckidx_ref.at[pt_ref[seq * n_pages + j]],
                                        kidx_buf.at[seq % 2, j], sem_i.at[seq % 2])
            cpy.start() if action == "start" else cpy.wait()
            return c

        lax.fori_loop(0, n_pages, body, 0)

    def kv_copies(g):
        return (_chunk_copies(ck_ref, kbuf, sem_k, pt_ref, g)
                + _chunk_copies(cv_ref, vbuf, sem_v, pt_ref, g))

    @pl.when(b == 0)
    def _():
        idx_pages(0, "start")

    _stream_prologue(kv_copies, n_total, lookahead)
    _pad_rows(knew_pad, knew_ref)
    _pad_rows(vnew_pad, vnew_ref)
    _pad_rows(kinew_pad, kinew_ref)
    idx_pages(b, "wait")

    @pl.when(b + 1 < n_seq)
    def _():
        idx_pages(b + 1, "start")

    qi2 = qi2_ref[...]
    wcol = wcol_ref[...]
    scale = (H_IDX * D_IDX) ** -0.5

    def head_sum(s):
        r = wcol * jnp.maximum(s, 0.0)
        acc = jnp.zeros((t_new, s.shape[1]), jnp.float32)
        for h in range(H_IDX):
            acc = acc + r[h * t_new:(h + 1) * t_new, :]
        return acc * scale

    pp = INDEXER_PAGES_PER_STEP

    def idx_body(c, carry):
        kk = jnp.concatenate([kidx_buf[b % 2, c * pp + p] for p in range(pp)], axis=1)
        sc = head_sum(_dot(qi2, kk.astype(MXU_DTYPE)))
        key_ref[:, pl.ds(pl.multiple_of(c * pp * page, pp * page), pp * page)] = _sortable_key(sc)
        return carry

    lax.fori_loop(0, n_pages // pp, idx_body, 0)
    sc_new = head_sum(_dot_nt(qi2, kinew_pad[...]))
    tok = lax.broadcasted_iota(jnp.int32, (t_new, LANES), 0)
    lane = lax.broadcasted_iota(jnp.int32, (t_new, LANES), 1)
    key_ref[:, past:past + kcs] = jnp.full((t_new, kcs), INT_MIN, jnp.int32)
    key_ref[:, past:past + LANES] = jnp.where(lane <= tok, _sortable_key(sc_new), int_min)
    thr = _select_topk(key_ref, (past + kcs) // kcs, kcs, k_sel, kt=min(kcs, 512))

    qblk = qblk_ref[...]

    def head_bias(key_cols):
        bias = jnp.where(key_cols >= thr, 0.0, NEG_INF)
        return jnp.concatenate([bias] * n_heads, axis=0)

    def kv_body(j, carry):
        m, l, acc = carry
        g = b * n_chunks + j
        _stream_wait(kv_copies, g, n_total, lookahead)
        kt = _chunk_transposed(kbuf, g % n_slots)
        vt = _chunk_transposed(vbuf, g % n_slots)
        cols = key_ref[:, pl.ds(pl.multiple_of(j * cp * page, cp * page), cp * page)]
        return _softmax_step(_dot(qblk, kt) + head_bias(cols), m, l, acc, vt,
                             v_is_transposed=True)

    rows = n_heads * t_new
    init = (jnp.full((rows, 1), M_INIT, jnp.float32), jnp.zeros((rows, 1), jnp.float32),
            jnp.zeros((rows, dd), jnp.float32))
    m, l, acc = lax.fori_loop(0, n_chunks, kv_body, init)
    s_new = _dot_nt(qblk, knew_pad[...]) + head_bias(key_ref[:, past:past + LANES])
    _, l, acc = _softmax_step(s_new, m, l, acc, vnew_pad[...])
    o_ref[...] = _heads_to_lanes(acc / l, n_heads, t_new)


def _block_diag_queries(qb, b, t, n_heads):
    q4 = qb.reshape(b, t, n_heads, HEAD_DIM)
    eye = jnp.eye(n_heads, dtype=qb.dtype)
    return jnp.einsum("bthd,hg->bhtgd", q4, eye).reshape(b, n_heads * t, n_heads * HEAD_DIM)


def _dsa_sample(qb, kb, vb, qib, kib2, wi, cache_k, cache_v, cache_kidx, page_table, b, t):
    dd = qb.shape[1]
    n_heads = dd // HEAD_DIM
    n_pool, page = cache_k.shape[0], cache_k.shape[1]
    n_pages = page_table.shape[1]
    past = n_pages * page
    k_sel = min(DSA_TOPK_MAX, (past + t) // 4)
    cp, n_slots = SAMPLE_PAGES_PER_CHUNK, SAMPLE_RING_SLOTS
    kcs = min(SAMPLE_SELECT_KC, past)
    assert n_pages % cp == 0 and past % kcs == 0 and t <= LANES and kcs >= k_sel
    assert n_pages % INDEXER_PAGES_PER_STEP == 0 and t == SUBLANES
    mx = MXU_DTYPE
    qblk = _block_diag_queries(qb, b, t, n_heads)
    qi2 = qib.reshape(b, t, H_IDX, D_IDX).transpose(0, 2, 1, 3).reshape(b, H_IDX * t, D_IDX)
    wcol = wi.reshape(b, t, H_IDX).transpose(0, 2, 1).reshape(b, H_IDX * t, 1)
    kinew = kib2[:, :D_IDX].reshape(b, t, D_IDX)
    per_b = lambda r, n: pl.BlockSpec((None, r, n), lambda bi, pt: (bi, 0, 0))
    anyspec = pl.BlockSpec(memory_space=pl.ANY)
    grid_spec = pltpu.PrefetchScalarGridSpec(
        num_scalar_prefetch=1, grid=(b,),
        in_specs=[per_b(n_heads * t, dd), per_b(H_IDX * t, D_IDX), per_b(H_IDX * t, 1),
                  per_b(t, dd), per_b(t, dd), per_b(t, D_IDX), anyspec, anyspec, anyspec],
        out_specs=per_b(t, dd),
        scratch_shapes=[
            pltpu.VMEM((2, n_pages, D_IDX, page), jnp.float32),
            pltpu.VMEM((t, past + kcs), jnp.int32),
            pltpu.VMEM((n_slots, cp, n_heads, HEAD_DIM, page), jnp.float32),
            pltpu.VMEM((n_slots, cp, n_heads, HEAD_DIM, page), jnp.float32),
            pltpu.VMEM((LANES, dd), mx), pltpu.VMEM((LANES, dd), mx),
            pltpu.VMEM((LANES, D_IDX), mx),
            pltpu.SemaphoreType.DMA((2,)), pltpu.SemaphoreType.DMA((n_slots,)),
            pltpu.SemaphoreType.DMA((n_slots,)),
        ])
    out = pl.pallas_call(
        functools.partial(_dsa_sample_body, k_sel, b, n_pages, kcs),
        grid_spec=grid_spec,
        out_shape=jax.ShapeDtypeStruct((b, t, dd), jnp.float32),
        compiler_params=_cparams(("arbitrary",)), name="dsa_sample",
    )(page_table.reshape(-1), qblk, qi2, wcol, kb.reshape(b, t, dd), vb.reshape(b, t, dd),
      kinew, _pages_transposed(cache_kidx), _pages_transposed(cache_k),
      _pages_transposed(cache_v))
    return out.reshape(b * t, dd)


def _moba_sample_body(topk, n_seq, n_pages, pt_ref, qblk_ref, knew_ref, vnew_ref, ck_ref,
                      cv_ref, o_ref, s_ref, kbuf, vbuf, knew_pad, vnew_pad, sem_k, sem_v):
    b = pl.program_id(0)
    n_slots, cp = kbuf.shape[0], kbuf.shape[1]
    lookahead = n_slots - 1
    page = kbuf.shape[4]
    dm = kbuf.shape[2] * kbuf.shape[3]
    n_chunks = n_pages // cp
    n_total = n_seq * n_chunks
    ck = cp * page
    blk = MOBA_BLOCK
    bpc = ck // blk
    nb = n_pages * page // blk
    t_new = knew_ref.shape[0]
    n_heads = qblk_ref.shape[0] // t_new
    rows = n_heads * t_new

    k_copies = lambda g: _chunk_copies(ck_ref, kbuf, sem_k, pt_ref, g)
    v_copies = lambda g: _chunk_copies(cv_ref, vbuf, sem_v, pt_ref, g)
    _stream_prologue(k_copies, n_total, lookahead)
    _stream_prologue(v_copies, n_total, lookahead)
    _pad_rows(knew_pad, knew_ref)
    _pad_rows(vnew_pad, vnew_ref)
    qblk = qblk_ref[...]

    gate_lane = lax.broadcasted_iota(jnp.int32, (rows, LANES), 1)

    def k_body(j, gate):
        g_k = b * n_chunks + j
        _stream_wait(k_copies, g_k, n_total, lookahead)
        s = _dot(qblk, _chunk_transposed(kbuf, g_k % n_slots))
        s_ref[:, pl.ds(pl.multiple_of(j * ck, ck), ck)] = s
        for r in range(bpc):
            g = jnp.mean(s[:, r * blk:(r + 1) * blk], axis=1, keepdims=True)
            gate = jnp.where(gate_lane == j * bpc + r, g, gate)
        return gate

    gate = lax.fori_loop(0, n_chunks, k_body, jnp.zeros((rows, LANES), jnp.float32))
    sel = _moba_select(gate, 0, nb, topk).astype(MXU_DTYPE)
    n_row = lax.broadcasted_iota(jnp.int32, (LANES, ck), 0)
    col = lax.broadcasted_iota(jnp.int32, (LANES, ck), 1)
    blk_in_chunk = jnp.zeros((LANES, ck), jnp.int32)
    for r in range(1, bpc):
        blk_in_chunk = blk_in_chunk + jnp.where(col >= r * blk, 1, 0)

    def v_body(j, carry):
        m, l, acc = carry
        g_v = b * n_chunks + j
        _stream_wait(v_copies, g_v, n_total, lookahead)
        expand = jnp.where(n_row == j * bpc + blk_in_chunk, 1.0, 0.0).astype(MXU_DTYPE)
        s = jnp.where(_dot(sel, expand) > 0.5,
                      s_ref[:, pl.ds(pl.multiple_of(j * ck, ck), ck)], NEG_INF)
        return _softmax_step(s, m, l, acc, _chunk_transposed(vbuf, g_v % n_slots),
                             v_is_transposed=True)

    init = (jnp.full((rows, 1), M_INIT, jnp.float32), jnp.zeros((rows, 1), jnp.float32),
            jnp.zeros((rows, dm), jnp.float32))
    m, l, acc = lax.fori_loop(0, n_chunks, v_body, init)
    lane = lax.broadcasted_iota(jnp.int32, (rows, LANES), 1)
    s_new = jnp.where(lane <= _row_token(n_heads, t_new), _dot_nt(qblk, knew_pad[...]), NEG_INF)
    _, l, acc = _softmax_step(s_new, m, l, acc, vnew_pad[...])
    o_ref[...] = _heads_to_lanes(acc / l, n_heads, t_new)


def _moba_sample(qb, kb, vb, cache_k, cache_v, page_table, b, t):
    dm = qb.shape[1]
    n_heads = dm // HEAD_DIM
    n_pool, page = cache_k.shape[0], cache_k.shape[1]
    n_pages = page_table.shape[1]
    past = n_pages * page
    cp, n_slots = SAMPLE_PAGES_PER_CHUNK, SAMPLE_RING_SLOTS
    assert past % MOBA_BLOCK == 0 and t <= MOBA_BLOCK and t == SUBLANES
    assert n_pages % cp == 0 and (cp * page) % MOBA_BLOCK == 0
    nb = past // MOBA_BLOCK
    assert nb <= LANES
    topk = min(MOBA_TOPK, nb)
    mx = MXU_DTYPE
    qblk = _block_diag_queries(qb, b, t, n_heads)
    per_b = lambda r, n: pl.BlockSpec((None, r, n), lambda bi, pt: (bi, 0, 0))
    anyspec = pl.BlockSpec(memory_space=pl.ANY)
    grid_spec = pltpu.PrefetchScalarGridSpec(
        num_scalar_prefetch=1, grid=(b,),
        in_specs=[per_b(n_heads * t, dm), per_b(t, dm), per_b(t, dm), anyspec, anyspec],
        out_specs=per_b(t, dm),
        scratch_shapes=[
            pltpu.VMEM((n_heads * t, past), jnp.float32),
            pltpu.VMEM((n_slots, cp, n_heads, HEAD_DIM, page), jnp.float32),
            pltpu.VMEM((n_slots, cp, n_heads, HEAD_DIM, page), jnp.float32),
            pltpu.VMEM((LANES, dm), mx), pltpu.VMEM((LANES, dm), mx),
            pltpu.SemaphoreType.DMA((n_slots,)), pltpu.SemaphoreType.DMA((n_slots,)),
        ])
    out = pl.pallas_call(
        functools.partial(_moba_sample_body, topk, b, n_pages),
        grid_spec=grid_spec,
        out_shape=jax.ShapeDtypeStruct((b, t, dm), jnp.float32),
        compiler_params=_cparams(("arbitrary",)), name="moba_sample",
    )(page_table.reshape(-1), qblk, kb.reshape(b, t, dm), vb.reshape(b, t, dm),
      _pages_transposed(cache_k), _pages_transposed(cache_v))
    return out.reshape(b * t, dm)


def _trunk(x, p, conv_prev, ffn_prev, dsa_fn, moba_fn, w0_pad, w_conv_a0, wout0_a, wout0_b,
           w_in1, w_out1, ln_mix_g, ln_mix_b, ln_ffn_g, ln_ffn_b, w_ffn_in, w_conv_ffn,
           w_ffn_out, w_ple, w_ple_gate, alpha, c, dd, dm, is_prompt):
    b, t, d = x.shape
    m = b * t
    x2 = x.reshape(m, d)
    p2 = p.reshape(p.shape[0], m, p.shape[-1])

    ya, qb, k0, kb, v0, vb, qib, ki0, kib2, wi, conv_new = _proj0(
        x, w0_pad, w_conv_a0, conv_prev, c, dd)
    yb = dsa_fn(qb, kb, vb, qib, kib2, wi, b, t)
    x2 = _out_ln(x2, [ya, yb], [wout0_a, wout0_b], ln_mix_g[0], ln_mix_b[0], alpha)
    x2, st0 = _ffn(x2, p2[0], b, t, w_ffn_in[0], w_conv_ffn[0], w_ffn_out[0], ln_ffn_g[0],
                   ln_ffn_b[0], w_ple_gate[0], w_ple[0], ffn_prev[0], alpha)

    outs = _proj1(x2.reshape(b, t, d), w_in1, dm, is_prompt)
    qb1, k1, kb1, v1, vb1 = outs[:5]
    yc = moba_fn(qb1, kb1, vb1, *outs[5:], b, t)
    x2 = _out_ln(x2, [yc], [w_out1], ln_mix_g[1], ln_mix_b[1], alpha)
    x2, st1 = _ffn(x2, p2[1], b, t, w_ffn_in[1], w_conv_ffn[1], w_ffn_out[1], ln_ffn_g[1],
                   ln_ffn_b[1], w_ple_gate[1], w_ple[1], ffn_prev[1], alpha)

    h_dsa, h_moba = dd // HEAD_DIM, dm // HEAD_DIM
    return (x2.reshape(b, t, d), k0.reshape(b, t, h_dsa, HEAD_DIM),
            v0.reshape(b, t, h_dsa, HEAD_DIM), ki0.reshape(b, t, D_IDX), conv_new,
            k1.reshape(b, t, h_moba, HEAD_DIM), v1.reshape(b, t, h_moba, HEAD_DIM),
            jnp.stack([st0, st1]))


def kernel(x_prompt, x_sample, cache_k0, cache_v0, cache_kidx0, state_conv0, cache_k1, cache_v1, state_ffn, page_table, p_prompt, p_sample, w_in0, w_conv_a0, w_out0, w_in1, w_out1, ln_mix_g, ln_mix_b, ln_ffn_g, ln_ffn_b, w_ffn_in, w_conv_ffn, w_ffn_out, w_ple, w_ple_gate):
    depth = w_ffn_in.shape[0]
    assert depth == 2
    alpha = (2 * depth) ** 0.25
    d = x_prompt.shape[-1]
    c = w_conv_a0.shape[1]
    dd = cache_k0.shape[2] * HEAD_DIM
    dm = cache_k1.shape[2] * HEAD_DIM
    f = w_conv_ffn.shape[2]
    mx = MXU_DTYPE

    o = 3 * c + 3 * dd + H_IDX * D_IDX
    assert o % LANES == 0 and w_in0.shape[1] == o + D_IDX + H_IDX
    w0 = w_in0.astype(mx)
    ki_cols = w0[:, o:o + D_IDX]
    w0_pad = jnp.concatenate(
        [w0[:, :o], ki_cols, ki_cols, w0[:, o + D_IDX:], jnp.zeros((d, LANES - H_IDX), mx)],
        axis=1)
    wout0 = w_out0.astype(mx)
    shared = (w0_pad, w_conv_a0, wout0[:c], wout0[c:], w_in1.astype(mx), w_out1.astype(mx),
              ln_mix_g, ln_mix_b, ln_ffn_g, ln_ffn_b, w_ffn_in.astype(mx), w_conv_ffn,
              w_ffn_out.astype(mx), w_ple.astype(mx), w_ple_gate.astype(mx), alpha, c, dd, dm)

    bp = x_prompt.shape[0]
    conv_zero = jnp.zeros((bp, CONV_W - 1, c), jnp.float32)
    ffn_zero = jnp.zeros((depth, bp, CONV_W - 1, f), jnp.float32)
    prompt = _trunk(x_prompt, p_prompt, conv_zero, ffn_zero, _dsa_prompt, _moba_prompt,
                    *shared, True)

    def dsa_s(qb, kb, vb, qib, kib2, wi, b, t):
        return _dsa_sample(qb, kb, vb, qib, kib2, wi, cache_k0, cache_v0, cache_kidx0,
                           page_table, b, t)

    def moba_s(qb, kb, vb, b, t):
        return _moba_sample(qb, kb, vb, cache_k1, cache_v1, page_table, b, t)

    sample = _trunk(x_sample, p_sample, state_conv0, state_ffn, dsa_s, moba_s, *shared, False)
    y_p, k0_p, v0_p, ki_p, conv_p, k1_p, v1_p, ffn_p = prompt
    y_s, k0_s, v0_s, ki_s, conv_s, k1_s, v1_s, ffn_s = sample
    return (y_p, y_s, k0_p, v0_p, ki_p, conv_p, k1_p, v1_p, ffn_p,
            k0_s, v0_s, ki_s, conv_s, k1_s, v1_s, ffn_s)
```

```python
import functools

import jax
import jax.numpy as jnp
from jax import lax
from jax.experimental import pallas as pl
from jax.experimental.pallas import tpu as pltpu

HEAD_DIM = 64
H_IDX = 8
D_IDX = 64
DSA_TOPK_MAX = 256
MOBA_BLOCK = 256
MOBA_TOPK = 3
CONV_W = 3
LN_EPS = 1e-5

LANES = 128
SUBLANES = 8
VMEM_LIMIT_BYTES = 56 * 1024 * 1024
ROW_TILE = 256
DSA_TQ = 256
DSA_KC = 512
DSA_KA = 1024
MOBA_KC = 1024
SAMPLE_PAGES_PER_CHUNK = 4
SAMPLE_RING_SLOTS = 4
SAMPLE_SELECT_KC = 2048
INDEXER_PAGES_PER_STEP = 2

MXU_DTYPE = jnp.bfloat16
SELECT_ROW_BLOCK = 128
INT_MIN = -(2 ** 31)
NEG_INF = float("-inf")
M_INIT = -(2.0 ** 100)
MASK_BIAS = -(2.0 ** 101)


def _cparams(semantics):
    return pltpu.CompilerParams(dimension_semantics=semantics,
                                vmem_limit_bytes=VMEM_LIMIT_BYTES)


def _resident(shape):
    nd = len(shape)
    return pl.BlockSpec(shape, lambda *_: (0,) * nd, pipeline_mode=pl.Buffered(1))


def _dot(a, b):
    return jnp.dot(a, b, preferred_element_type=jnp.float32)


def _dot_nt(a, b):
    return lax.dot_general(a, b, (((1,), (1,)), ((), ())),
                           preferred_element_type=jnp.float32)


def _layer_norm(z, g, b):
    mu = jnp.mean(z, axis=-1, keepdims=True)
    var = jnp.mean(jnp.square(z - mu), axis=-1, keepdims=True)
    return (z - mu) * lax.rsqrt(var + LN_EPS) * g + b


def _conv3_carry(u, w_ref, ubuf_ref, prev_ref, first_tile):
    tm = u.shape[0]

    @pl.when(first_tile)
    def _():
        ubuf_ref[SUBLANES - 2:SUBLANES, :] = prev_ref[0]

    ubuf_ref[SUBLANES:SUBLANES + tm, :] = u
    u1 = ubuf_ref[SUBLANES - 1:SUBLANES - 1 + tm, :]
    u2 = ubuf_ref[SUBLANES - 2:SUBLANES - 2 + tm, :]
    y = u2 * w_ref[0:1, :] + u1 * w_ref[1:2, :] + u * w_ref[2:3, :]
    ubuf_ref[0:SUBLANES, :] = ubuf_ref[tm:tm + SUBLANES, :]
    return y


def _conv3_rows(u, w_ref, ubuf_ref, p1_ref, p2_ref, tpos_ref):
    tm = u.shape[0]
    ubuf_ref[0:SUBLANES, :] = jnp.zeros((SUBLANES, u.shape[1]), jnp.float32)
    ubuf_ref[SUBLANES:SUBLANES + tm, :] = u
    t = tpos_ref[...]
    u1 = jnp.where(t >= 1, ubuf_ref[SUBLANES - 1:SUBLANES - 1 + tm, :], p1_ref[...])
    u2 = jnp.where(t >= 2, ubuf_ref[SUBLANES - 2:SUBLANES - 2 + tm, :], p2_ref[...])
    return u2 * w_ref[0:1, :] + u1 * w_ref[1:2, :] + u * w_ref[2:3, :]


def _conv_prev_rows(prev, t_len):
    b, _, c = prev.shape
    z = jnp.zeros((b, t_len, c), jnp.float32)
    p1 = z.at[:, 0].set(prev[:, 1])
    p2 = z.at[:, 0].set(prev[:, 0]).at[:, 1].set(prev[:, 1])
    return p1.reshape(b * t_len, c), p2.reshape(b * t_len, c)


def _tpos(b, t_len):
    return jnp.tile(jnp.arange(t_len, dtype=jnp.int32), b).reshape(b * t_len, 1)


def _proj0_body(carry_mode, c, dd, x_ref, w_ref, wc_ref, *rest):
    if carry_mode:
        (prev_ref, ya_ref, qb_ref, k_ref, kb_ref, v_ref, vb_ref, qib_ref, ki_ref,
         kib2_ref, wi_ref, st_ref, ubuf_ref) = rest
    else:
        (p1_ref, p2_ref, tpos_ref, ya_ref, qb_ref, k_ref, kb_ref, v_ref, vb_ref,
         qib_ref, ki_ref, kib2_ref, wi_ref, u_ref, ubuf_ref) = rest
    res = _dot(x_ref[...].astype(MXU_DTYPE), w_ref[...])
    bg = res[:, 0:c]
    u = res[:, c:2 * c] * res[:, 2 * c:3 * c]
    if carry_mode:
        y = _conv3_carry(u, wc_ref, ubuf_ref, prev_ref, pl.program_id(1) == 0)
        st_ref[0] = u[u.shape[0] - 2:, :]
    else:
        y = _conv3_rows(u, wc_ref, ubuf_ref, p1_ref, p2_ref, tpos_ref)
        u_ref[...] = u
    ya_ref[...] = bg * y
    o = 3 * c
    q = res[:, o:o + dd]
    k = res[:, o + dd:o + 2 * dd]
    v = res[:, o + 2 * dd:o + 3 * dd]
    qb_ref[...] = (q * HEAD_DIM ** -0.5).astype(MXU_DTYPE)
    k_ref[...] = k
    kb_ref[...] = k.astype(MXU_DTYPE)
    v_ref[...] = v
    vb_ref[...] = v.astype(MXU_DTYPE)
    o += 3 * dd
    di = H_IDX * D_IDX
    qib_ref[...] = res[:, o:o + di].astype(MXU_DTYPE)
    o += di
    ki_ref[...] = res[:, o:o + D_IDX]
    kib2_ref[...] = res[:, o:o + 2 * D_IDX].astype(MXU_DTYPE)
    o += 2 * D_IDX
    wi_ref[...] = res[:, o:o + H_IDX]


def _proj0(x, w_pad, w_conv, prev, c, dd):
    b, t, d = x.shape
    m = b * t
    npad = w_pad.shape[1]
    di = H_IDX * D_IDX
    carry_mode = t % ROW_TILE == 0
    x2 = x.reshape(m, d)
    f32, mx = jnp.float32, MXU_DTYPE
    row_shapes = [(c, f32), (dd, mx), (dd, f32), (dd, mx), (dd, f32), (dd, mx),
                  (di, mx), (D_IDX, f32), (2 * D_IDX, mx), (H_IDX, f32)]
    if carry_mode:
        tm = ROW_TILE
        nt = t // tm
        grid = (b, nt)
        row = lambda n: pl.BlockSpec((tm, n), lambda bi, j: (bi * nt + j, 0))
        in_specs = [row(d), _resident((d, npad)), _resident((CONV_W, c)),
                    pl.BlockSpec((1, CONV_W - 1, c), lambda bi, j: (bi, 0, 0))]
        out_specs = [row(n) for n, _ in row_shapes] + [
            pl.BlockSpec((1, CONV_W - 1, c), lambda bi, j: (bi, 0, 0))]
        out_shape = [jax.ShapeDtypeStruct((m, n), dt) for n, dt in row_shapes] + [
            jax.ShapeDtypeStruct((b, CONV_W - 1, c), f32)]
        args = (x2, w_pad, w_conv, prev)
        sem = ("arbitrary", "arbitrary")
    else:
        assert m % SUBLANES == 0 and t >= CONV_W - 1
        tm = m
        grid = (1,)
        row = lambda n: pl.BlockSpec((tm, n), lambda i: (0, 0))
        p1, p2 = _conv_prev_rows(prev, t)
        in_specs = [row(d), _resident((d, npad)), _resident((CONV_W, c)),
                    row(c), row(c), row(1)]
        out_specs = [row(n) for n, _ in row_shapes] + [row(c)]
        out_shape = [jax.ShapeDtypeStruct((m, n), dt) for n, dt in row_shapes] + [
            jax.ShapeDtypeStruct((m, c), f32)]
        args = (x2, w_pad, w_conv, p1, p2, _tpos(b, t))
        sem = ("arbitrary",)
    outs = pl.pallas_call(
        functools.partial(_proj0_body, carry_mode, c, dd),
        grid=grid, in_specs=in_specs, out_specs=out_specs, out_shape=out_shape,
        scratch_shapes=[pltpu.VMEM((tm + 2 * SUBLANES, c), f32)],
        compiler_params=_cparams(sem), name="proj0",
    )(*args)
    outs = list(outs)
    if not carry_mode:
        outs[-1] = outs[-1].reshape(b, t, c)[:, t - (CONV_W - 1):, :]
    return outs


def _proj1_body(blocks_per_seq, dm, x_ref, w_ref, qb_ref, k_ref, ka_ref, v_ref, va_ref, *rest):
    res = _dot(x_ref[...].astype(MXU_DTYPE), w_ref[...])
    k = res[:, dm:2 * dm]
    v = res[:, 2 * dm:3 * dm]
    qb_ref[...] = (res[:, 0:dm] * HEAD_DIM ** -0.5).astype(MXU_DTYPE)
    k_ref[...] = k
    v_ref[...] = v
    if blocks_per_seq is None:
        ka_ref[...] = k.astype(MXU_DTYPE)
        va_ref[...] = v.astype(MXU_DTYPE)
        return
    rest[0][0] = jnp.mean(k, axis=0, keepdims=True)
    blk = pl.program_id(0) % blocks_per_seq
    lane = lax.broadcasted_iota(jnp.int32, (k.shape[0], LANES), 1)
    low = lane < HEAD_DIM
    e_high = jnp.where(lane == HEAD_DIM + blk, MASK_BIAS, 0.0)
    e_low = jnp.where(lane == blk, MASK_BIAS, 0.0)
    for hp in range(dm // LANES):
        kp = k[:, hp * LANES:(hp + 1) * LANES]
        vp = v[:, hp * LANES:(hp + 1) * LANES]
        even = slice(2 * hp * LANES, (2 * hp + 1) * LANES)
        odd = slice((2 * hp + 1) * LANES, (2 * hp + 2) * LANES)
        ka_ref[:, even] = jnp.where(low, kp, e_high).astype(MXU_DTYPE)
        ka_ref[:, odd] = jnp.where(low, e_low, kp).astype(MXU_DTYPE)
        va_ref[:, even] = jnp.where(low, vp, 1.0).astype(MXU_DTYPE)
        va_ref[:, odd] = jnp.where(low, 1.0, vp).astype(MXU_DTYPE)


def _proj1(x, w, dm, is_prompt):
    b, t, d = x.shape
    m = b * t
    f32, mx = jnp.float32, MXU_DTYPE
    if is_prompt:
        assert t % MOBA_BLOCK == 0 and t // MOBA_BLOCK <= HEAD_DIM
        tm = MOBA_BLOCK
        da = 2 * dm
        blocks_per_seq = t // MOBA_BLOCK
    else:
        tm = m
        da = dm
        blocks_per_seq = None
    nt = m // tm
    row = lambda n: pl.BlockSpec((tm, n), lambda i: (i, 0))
    out_specs = [row(dm), row(dm), row(da), row(dm), row(da)]
    out_shape = [jax.ShapeDtypeStruct((m, n), dt)
                 for n, dt in ((dm, mx), (dm, f32), (da, mx), (dm, f32), (da, mx))]
    if is_prompt:
        out_specs.append(pl.BlockSpec((1, 1, dm), lambda i: (i, 0, 0)))
        out_shape.append(jax.ShapeDtypeStruct((nt, 1, dm), f32))
    return pl.pallas_call(
        functools.partial(_proj1_body, blocks_per_seq, dm),
        grid=(nt,), in_specs=[row(d), _resident((d, 3 * dm))],
        out_specs=out_specs, out_shape=out_shape,
        compiler_params=_cparams(("arbitrary",)), name="proj1",
    )(x.reshape(m, d), w)


def _out_ln_body(n_act, alpha, x_ref, *rest):
    acts = rest[:n_act]
    ws = rest[n_act:2 * n_act]
    g_ref, b_ref, o_ref = rest[2 * n_act:]
    y = _dot(acts[0][...].astype(MXU_DTYPE), ws[0][...])
    for a_ref, w_ref in zip(acts[1:], ws[1:]):
        y = y + _dot(a_ref[...].astype(MXU_DTYPE), w_ref[...])
    o_ref[...] = _layer_norm(alpha * x_ref[...] + y, g_ref[...], b_ref[...])


def _out_ln(x2, acts, ws, g, bta, alpha):
    m, d = x2.shape
    tm = ROW_TILE if m % ROW_TILE == 0 else m
    row = lambda n: pl.BlockSpec((tm, n), lambda i: (i, 0))
    in_specs = ([row(d)] + [row(a.shape[1]) for a in acts]
                + [_resident(w.shape) for w in ws] + [_resident((1, d))] * 2)
    return pl.pallas_call(
        functools.partial(_out_ln_body, len(acts), alpha),
        grid=(m // tm,), in_specs=in_specs, out_specs=row(d),
        out_shape=jax.ShapeDtypeStruct((m, d), jnp.float32),
        compiler_params=_cparams(("arbitrary",)), name="out_ln",
    )(x2, *acts, *ws, g.reshape(1, d), bta.reshape(1, d))


def _ffn_body(carry_mode, f, alpha, x_ref, p_ref, win_ref, wc_ref, wout_ref, g_ref, b_ref,
              wg_ref, wp_ref, *rest):
    if carry_mode:
        prev_ref, o_ref, st_ref, ubuf_ref = rest
    else:
        p1_ref, p2_ref, tpos_ref, o_ref, u_ref, ubuf_ref = rest
    x = x_ref[...]
    h = _dot(x.astype(MXU_DTYPE), win_ref[...])
    u = h[:, 0:f]
    if carry_mode:
        uc = _conv3_carry(u, wc_ref, ubuf_ref, prev_ref, pl.program_id(1) == 0)
        st_ref[0] = u[u.shape[0] - 2:, :]
    else:
        uc = _conv3_rows(u, wc_ref, ubuf_ref, p1_ref, p2_ref, tpos_ref)
        u_ref[...] = u
    a = jax.nn.gelu(uc) * h[:, f:2 * f]
    y = _dot(a.astype(MXU_DTYPE), wout_ref[...])
    x2 = _layer_norm(alpha * x + y, g_ref[...], b_ref[...])
    gate = jax.nn.sigmoid(_dot(x2.astype(MXU_DTYPE), wg_ref[...]))
    o_ref[...] = x2 + gate * _dot(p_ref[...].astype(MXU_DTYPE), wp_ref[...])


def _ffn(x2, p2d, b, t, win, wc, wout, g, bta, wg, wp, prev, alpha):
    m, d = x2.shape
    f = wc.shape[1]
    dp = p2d.shape[1]
    carry_mode = t % ROW_TILE == 0
    f32 = jnp.float32
    weights = [_resident((d, 2 * f)), _resident((CONV_W, f)), _resident((f, d)),
               _resident((1, d)), _resident((1, d)), _resident((d, d)), _resident((dp, d))]
    if carry_mode:
        tm = ROW_TILE
        nt = t // tm
        grid = (b, nt)
        row = lambda n: pl.BlockSpec((tm, n), lambda bi, j: (bi * nt + j, 0))
        st_spec = pl.BlockSpec((1, CONV_W - 1, f), lambda bi, j: (bi, 0, 0))
        in_specs = [row(d), row(dp)] + weights + [st_spec]
        out_specs = [row(d), st_spec]
        out_shape = [jax.ShapeDtypeStruct((m, d), f32),
                     jax.ShapeDtypeStruct((b, CONV_W - 1, f), f32)]
        extra = (prev,)
        sem = ("arbitrary", "arbitrary")
    else:
        tm = m
        grid = (1,)
        row = lambda n: pl.BlockSpec((tm, n), lambda i: (0, 0))
        p1, p2 = _conv_prev_rows(prev, t)
        in_specs = [row(d), row(dp)] + weights + [row(f), row(f), row(1)]
        out_specs = [row(d), row(f)]
        out_shape = [jax.ShapeDtypeStruct((m, d), f32), jax.ShapeDtypeStruct((m, f), f32)]
        extra = (p1, p2, _tpos(b, t))
        sem = ("arbitrary",)
    xo, st = pl.pallas_call(
        functools.partial(_ffn_body, carry_mode, f, alpha),
        grid=grid, in_specs=in_specs, out_specs=out_specs, out_shape=out_shape,
        scratch_shapes=[pltpu.VMEM((tm + 2 * SUBLANES, f), f32)],
        compiler_params=_cparams(sem), name="ffn",
    )(x2, p2d, win, wc, wout, g.reshape(1, d), bta.reshape(1, d), wg, wp, *extra)
    if not carry_mode:
        st = st.reshape(b, t, f)[:, t - (CONV_W - 1):, :]
    return xo, st


def _sortable_key(sc):
    bits = pltpu.bitcast(sc, jnp.int32)
    return bits ^ ((bits >> 31) & 0x7FFFFFFF)


def _fold_lanes(x):
    part = x[:, 0:LANES]
    for j in range(1, x.shape[1] // LANES):
        part = part + x[:, j * LANES:(j + 1) * LANES]
    return part


def _row_blocks(rows):
    rb = min(rows, SELECT_ROW_BLOCK)
    assert rows % rb == 0
    return [(r0, rb) for r0 in range(0, rows, rb)]


def _count_ge(key_ref, cand_ref, cand, n_chunks, kc):
    blocks = _row_blocks(key_ref.shape[0])
    cand_ref[...] = jnp.broadcast_to(cand, cand_ref.shape)

    def body(c, accs):
        out = []
        for (r0, rb), acc in zip(blocks, accs):
            cr = cand_ref[r0:r0 + rb, :]
            kk = key_ref[r0:r0 + rb, pl.ds(pl.multiple_of(c * kc, kc), kc)]
            for j in range(kc // LANES):
                acc = acc + jnp.where(kk[:, j * LANES:(j + 1) * LANES] >= cr, 1.0, 0.0)
            out.append(acc)
        return tuple(out)

    init = tuple(jnp.zeros((rb, LANES), jnp.float32) for _, rb in blocks)
    accs = lax.fori_loop(0, n_chunks, body, init)
    acc = accs[0] if len(accs) == 1 else jnp.concatenate(accs, axis=0)
    return jnp.sum(acc, axis=1, keepdims=True)


def _select_topk(key_ref, cand_ref, n_chunks, kc, k_sel, kt=None):
    rows = key_ref.shape[0]
    int_min = jnp.int32(INT_MIN)
    kt = kc if kt is None else kt
    assert kc % kt == 0

    def bit_body(i, t_u):
        cand_u = t_u | lax.shift_left(jnp.int32(1), 31 - i)
        cnt = _count_ge(key_ref, cand_ref, cand_u ^ int_min, n_chunks, kc)
        return jnp.where(cnt >= k_sel, cand_u, t_u)

    t_u = lax.fori_loop(0, 32, bit_body, jnp.zeros((rows, 1), jnp.int32))
    thr = jnp.maximum(t_u ^ int_min, int_min + 1)
    cnt_ge = _count_ge(key_ref, cand_ref, thr, n_chunks, kc)
    excess = jnp.max(jnp.where(cnt_ge > k_sel, 1.0, 0.0))

    @pl.when(excess > 0.0)
    def _():
        cnt_gt = _count_ge(key_ref, cand_ref, thr + 1, n_chunks, kc)
        need = k_sel - cnt_gt
        r_i = lax.broadcasted_iota(jnp.int32, (kt, kt), 0)
        c_i = lax.broadcasted_iota(jnp.int32, (kt, kt), 1)
        tri = jnp.where(r_i <= c_i, 1.0, 0.0).astype(MXU_DTYPE)

        def body(c, seen):
            sl = pl.ds(pl.multiple_of(c * kt, kt), kt)
            kk = key_ref[:, sl]
            tie = kk == thr
            tie_f = jnp.where(tie, 1.0, 0.0)
            rank = _dot(tie_f.astype(MXU_DTYPE), tri) + seen
            key_ref[:, sl] = jnp.where(tie & (rank > need), int_min, kk)
            return seen + jnp.sum(tie_f, axis=1, keepdims=True)

        lax.fori_loop(0, n_chunks * (kc // kt), body, jnp.zeros((rows, 1), jnp.float32))

    return thr


def _softmax_step(s, m, l, acc, v, v_is_transposed=False):
    m_new = jnp.maximum(m, jnp.max(s, axis=1, keepdims=True))
    a = jnp.exp(m - m_new)
    p = jnp.exp(s - m_new)
    l_new = a * l + jnp.sum(p, axis=1, keepdims=True)
    pv = (_dot_nt if v_is_transposed else _dot)(p.astype(MXU_DTYPE), v)
    return m_new, l_new, a * acc + pv


def _head_halves(x):
    lane = lax.broadcasted_iota(jnp.int32, x.shape, 1)
    zero = jnp.zeros_like(x)
    return jnp.where(lane < HEAD_DIM, x, zero), jnp.where(lane >= HEAD_DIM, x, zero)


def _dsa_prompt_body(k_sel, qb_ref, kb_ref, vb_ref, qib_ref, kib2_ref, wi_ref, o_ref,
                     key_ref, cand_ref, thr_ref, m_ref, l_ref, acc_ref, sa_ref, sb_ref):
    i = pl.program_id(1)
    hp = pl.program_id(2)
    tq = qb_ref.shape[0]
    kc = DSA_KC
    ka = min(DSA_KA, kb_ref.shape[0])
    n_att = ((i + 1) * tq + ka - 1) // ka
    n_chunks = n_att * (ka // kc)
    qpos = i * tq + lax.broadcasted_iota(jnp.int32, (tq, 1), 0)

    @pl.when(hp == 0)
    def _():
        qi = qib_ref[...]
        w = wi_ref[...]
        qm = []
        for hh in range(H_IDX // 2):
            qm.extend(_head_halves(qi[:, hh * LANES:(hh + 1) * LANES]))
        qstack = jnp.concatenate(qm, axis=0)
        wcol = [w[:, h:h + 1] for h in range(H_IDX)]

        def idx_body(c, carry):
            sl = pl.ds(pl.multiple_of(c * kc, kc), kc)
            s_all = _dot_nt(qstack, kib2_ref[sl, :])
            acc = jnp.zeros((tq, kc), jnp.float32)
            for h in range(H_IDX):
                acc = acc + wcol[h] * jnp.maximum(s_all[h * tq:(h + 1) * tq, :], 0.0)
            sc = acc * (H_IDX * D_IDX) ** -0.5
            kpos = c * kc + lax.broadcasted_iota(jnp.int32, (1, kc), 1)
            key_ref[:, sl] = jnp.where(kpos <= qpos, _sortable_key(sc), jnp.int32(INT_MIN))
            return carry

        lax.fori_loop(0, n_chunks, idx_body, 0)
        thr_ref[...] = _select_topk(key_ref, cand_ref, n_att, ka, k_sel, kt=kc)

    thr = thr_ref[...]
    qs = _head_halves(qb_ref[...])
    m_ref[...] = jnp.full(m_ref.shape, M_INIT, jnp.float32)
    l_ref[...] = jnp.zeros(l_ref.shape, jnp.float32)
    acc_ref[...] = jnp.zeros(acc_ref.shape, jnp.float32)

    def scores_into(s_ref, c):
        rows = pl.ds(pl.multiple_of(c * ka, ka), ka)
        kp = kb_ref[rows, :]
        sel = key_ref[:, rows] >= thr
        for h in range(2):
            s_ref[h] = jnp.where(sel, _dot_nt(qs[h], kp), MASK_BIAS)

    def consume(s_ref, c):
        vp = vb_ref[pl.ds(pl.multiple_of(c * ka, ka), ka), :]
        for h in range(2):
            s = s_ref[h]
            m_old = m_ref[h]
            m_new = jnp.maximum(m_old, jnp.max(s, axis=1, keepdims=True))
            a = jnp.exp(m_old - m_new)
            p = jnp.exp(s - m_new)
            l_ref[h] = a * l_ref[h] + jnp.sum(p, axis=1, keepdims=True)
            acc_ref[h] = a * acc_ref[h] + _dot(p.astype(MXU_DTYPE), vp)
            m_ref[h] = m_new

    _two_stage_loop(n_att, scores_into, consume, sa_ref, sb_ref)
    lane = lax.broadcasted_iota(jnp.int32, (tq, LANES), 1)
    o_ref[...] = jnp.where(lane < HEAD_DIM, acc_ref[0] / l_ref[0], acc_ref[1] / l_ref[1])


def _dsa_prompt(qb, kb, vb, qib, kib2, wi, b, t):
    dd = qb.shape[1]
    k_sel = min(DSA_TOPK_MAX, t // 4)
    tq, kc, ka = DSA_TQ, DSA_KC, min(DSA_KA, t)
    assert t % ka == 0 and ka % kc == 0 and ka % tq == 0 and kc >= k_sel and dd % LANES == 0
    nq, n_hp = t // tq, dd // LANES
    r3 = lambda a: a.reshape(b, t, a.shape[1])
    di = qib.shape[1]
    f32 = jnp.float32
    out = pl.pallas_call(
        functools.partial(_dsa_prompt_body, k_sel),
        grid=(b, nq, n_hp),
        in_specs=[
            pl.BlockSpec((None, tq, LANES), lambda bi, i, hp: (bi, i, hp)),
            pl.BlockSpec((None, t, LANES), lambda bi, i, hp: (bi, 0, hp)),
            pl.BlockSpec((None, t, LANES), lambda bi, i, hp: (bi, 0, hp)),
            pl.BlockSpec((None, tq, di), lambda bi, i, hp: (bi, i, 0)),
            pl.BlockSpec((None, t, 2 * D_IDX), lambda bi, i, hp: (bi, 0, 0),
                         pipeline_mode=pl.Buffered(1)),
            pl.BlockSpec((None, tq, H_IDX), lambda bi, i, hp: (bi, i, 0)),
        ],
        out_specs=pl.BlockSpec((None, tq, LANES), lambda bi, i, hp: (bi, i, hp)),
        out_shape=jax.ShapeDtypeStruct((b, t, dd), f32),
        scratch_shapes=[pltpu.VMEM((tq, t), jnp.int32), pltpu.VMEM((tq, LANES), jnp.int32),
                        pltpu.VMEM((tq, 1), jnp.int32),
                        pltpu.VMEM((2, tq, 1), f32), pltpu.VMEM((2, tq, 1), f32),
                        pltpu.VMEM((2, tq, LANES), f32),
                        pltpu.VMEM((2, tq, ka), f32), pltpu.VMEM((2, tq, ka), f32)],
        compiler_params=_cparams(("arbitrary", "arbitrary", "arbitrary")),
        name="dsa_prompt",
    )(r3(qb), r3(kb), r3(vb), r3(qib), r3(kib2), r3(wi))
    return out.reshape(b * t, dd)


def _moba_select(gate, first_lane, n_valid, topk):
    rows, n = gate.shape
    n_idx = lax.broadcasted_iota(jnp.int32, (rows, n), 1)
    n_f = n_idx.astype(jnp.float32)

    def in_window(x, other):
        return jnp.where(n_idx >= first_lane,
                         jnp.where(n_idx < first_lane + n_valid, x, other), other)

    g = in_window(gate, NEG_INF)
    picked = jnp.zeros((rows, n), jnp.float32)
    for _ in range(topk):
        mx = jnp.max(g, axis=1, keepdims=True)
        first = jnp.min(jnp.where(g == mx, n_f, float(n)), axis=1, keepdims=True)
        hit = n_f == first
        picked = jnp.where(hit, 1.0, picked)
        g = jnp.where(hit, NEG_INF, g)
    return in_window(picked, 0.0)


def _two_stage_loop(n, produce, consume, buf_a, buf_b):
    last = jnp.maximum(n - 1, 0)

    @pl.when(n > 0)
    def _():
        produce(buf_a, 0)

    def pair(p, carry):
        c0 = 2 * p
        produce(buf_b, jnp.minimum(c0 + 1, last))
        consume(buf_a, c0)

        @pl.when(c0 + 1 < n)
        def _():
            produce(buf_a, jnp.minimum(c0 + 2, last))
            consume(buf_b, c0 + 1)

        return carry

    lax.fori_loop(0, (n + 1) // 2, pair, 0)


def _moba_prompt_body(topk, qb_ref, ka_ref, va_ref, kmx_ref, o_ref, m_ref, acc_ref,
                      sa_ref, sb_ref):
    i = pl.program_id(2)
    tq = qb_ref.shape[0]
    blk = MOBA_BLOCK
    kc = min(MOBA_KC, ka_ref.shape[0])
    q2 = qb_ref[...]
    lane = lax.broadcasted_iota(jnp.int32, (tq, LANES), 1)
    low = lane < HEAD_DIM
    q0, q1 = _head_halves(q2)
    kmx = kmx_ref[...].astype(MXU_DTYPE)
    pick0 = _moba_select(_dot_nt(q0, kmx[0:LANES]), HEAD_DIM, i, topk)
    pick1 = _moba_select(_dot_nt(q1, kmx[LANES:2 * LANES]), 0, i, topk)
    zero = jnp.zeros_like(q2)
    qa = (jnp.where(low, q2, (1.0 - pick0).astype(MXU_DTYPE)),
          jnp.where(low, (1.0 - pick1).astype(MXU_DTYPE), q2))
    m_ref[...] = jnp.full(m_ref.shape, M_INIT, jnp.float32)
    acc_ref[...] = jnp.zeros(acc_ref.shape, jnp.float32)

    def update(h, s, v):
        m_old = m_ref[h]
        m_new = jnp.maximum(m_old, jnp.max(s, axis=1, keepdims=True))
        p = jnp.exp(s - m_new).astype(MXU_DTYPE)
        acc_ref[h] = jnp.exp(m_old - m_new) * acc_ref[h] + _dot(p, v)
        m_ref[h] = m_new

    def scores_into(s_ref, c):
        rows = pl.ds(pl.multiple_of(c * kc, kc), kc)
        for h in range(2):
            s_ref[h] = _dot_nt(qa[h], ka_ref[rows, h * LANES:(h + 1) * LANES])

    def consume(s_ref, c):
        rows = pl.ds(pl.multiple_of(c * kc, kc), kc)
        for h in range(2):
            update(h, s_ref[h], va_ref[rows, h * LANES:(h + 1) * LANES])

    _two_stage_loop((i * blk + kc - 1) // kc, scores_into, consume, sa_ref, sb_ref)

    rows = pl.ds(pl.multiple_of(i * blk, blk), blk)
    causal = (lax.broadcasted_iota(jnp.int32, (tq, blk), 1)
              <= lax.broadcasted_iota(jnp.int32, (tq, blk), 0))
    own_lane = (HEAD_DIM + i, i)
    for h in range(2):
        cols = slice(h * LANES, (h + 1) * LANES)
        q_own = jnp.where(lane == own_lane[h], zero, qa[h])
        s = jnp.where(causal, _dot_nt(q_own, ka_ref[rows, cols]), MASK_BIAS)
        update(h, s, va_ref[rows, cols])
    a0, a1 = acc_ref[0], acc_ref[1]
    o_ref[...] = jnp.where(low, a0 / pltpu.roll(a0, HEAD_DIM, 1), a1 / pltpu.roll(a1, HEAD_DIM, 1))


def _moba_prompt(qb, ka, va, kmean, b, t):
    dm = qb.shape[1]
    blk = MOBA_BLOCK
    nb = t // blk
    n_hp = dm // LANES
    assert t % blk == 0 and dm % LANES == 0 and nb <= HEAD_DIM and t % min(MOBA_KC, t) == 0
    topk = min(MOBA_TOPK, nb - 1)
    kmp = kmean.reshape(b, nb, n_hp, LANES).transpose(0, 2, 1, 3)
    kmp = jnp.pad(kmp, ((0, 0), (0, 0), (0, HEAD_DIM - nb), (0, 0)))
    z = jnp.zeros_like(kmp)
    kmx = jnp.concatenate([z, kmp, kmp, z], axis=2)
    resident = lambda n: pl.BlockSpec((None, t, n), lambda bi, hp, i: (bi, 0, hp),
                                      pipeline_mode=pl.Buffered(1))
    out = pl.pallas_call(
        functools.partial(_moba_prompt_body, topk),
        grid=(b, n_hp, nb),
        in_specs=[
            pl.BlockSpec((None, blk, LANES), lambda bi, hp, i: (bi, i, hp)),
            resident(2 * LANES), resident(2 * LANES),
            pl.BlockSpec((None, None, 2 * LANES, LANES), lambda bi, hp, i: (bi, hp, 0, 0)),
        ],
        out_specs=pl.BlockSpec((None, blk, LANES), lambda bi, hp, i: (bi, i, hp)),
        out_shape=jax.ShapeDtypeStruct((b, t, dm), jnp.float32),
        scratch_shapes=[pltpu.VMEM((2, blk, 1), jnp.float32),
                        pltpu.VMEM((2, blk, LANES), jnp.float32),
                        pltpu.VMEM((2, blk, min(MOBA_KC, t)), jnp.float32),
                        pltpu.VMEM((2, blk, min(MOBA_KC, t)), jnp.float32)],
        compiler_params=_cparams(("arbitrary", "arbitrary", "arbitrary")),
        name="moba_prompt",
    )(qb.reshape(b, t, dm), ka.reshape(b, t, 2 * dm), va.reshape(b, t, 2 * dm), kmx)
    return out.reshape(b * t, dm)


def _chunk_copies(cache_ref, buf_ref, sem, pt_ref, g):
    n_slots, cp = buf_ref.shape[0], buf_ref.shape[1]
    slot = g % n_slots
    return [pltpu.make_async_copy(cache_ref.at[pt_ref[g * cp + p]],
                                  buf_ref.at[slot, p], sem.at[slot])
            for p in range(cp)]


def _stream_prologue(copies, n_total, lookahead):
    assert n_total >= lookahead

    @pl.when(pl.program_id(0) == 0)
    def _():
        for g in range(lookahead):
            for cpy in copies(g):
                cpy.start()


def _stream_wait(copies, g, n_total, lookahead):
    for cpy in copies(g):
        cpy.wait()

    @pl.when(g + lookahead < n_total)
    def _():
        for cpy in copies(g + lookahead):
            cpy.start()


def _chunk_transposed(buf_ref, slot):
    cp, nh, dh, page = buf_ref.shape[1:]
    return jnp.concatenate([buf_ref[slot, p].reshape(nh * dh, page) for p in range(cp)],
                           axis=1).astype(MXU_DTYPE)


def _pages_transposed(cache):
    nd = cache.ndim
    return jnp.transpose(cache, (0,) + tuple(range(2, nd)) + (1,))


def _pad_rows(dst_ref, src_ref):
    dst_ref[...] = jnp.zeros(dst_ref.shape, dst_ref.dtype)
    dst_ref[0:src_ref.shape[0], :] = src_ref[...]


def _row_token(n_heads, t_new):
    return lax.broadcasted_iota(jnp.int32, (n_heads, t_new, LANES), 1).reshape(
        n_heads * t_new, LANES)


def _heads_to_lanes(x, n_heads, t_new):
    lane_head = lax.broadcasted_iota(jnp.int32, (t_new, x.shape[1]), 1) // HEAD_DIM
    out = jnp.zeros((t_new, x.shape[1]), jnp.float32)
    for h in range(n_heads):
        out = jnp.where(lane_head == h, x[h * t_new:(h + 1) * t_new, :], out)
    return out


def _dsa_sample_body(k_sel, n_seq, n_pages, kcs, pt_ref, qblk_ref, qi2_ref, wcol_ref,
                     knew_ref, vnew_ref, kinew_ref, ckidx_ref, ck_ref, cv_ref, o_ref,
                     kidx_buf, key_ref, cand_ref, kbuf, vbuf, knew_pad, vnew_pad, kinew_pad,
                     sem_i, sem_k, sem_v):
    b = pl.program_id(0)
    n_slots, cp = kbuf.shape[0], kbuf.shape[1]
    lookahead = n_slots - 1
    page = kbuf.shape[4]
    dd = kbuf.shape[2] * kbuf.shape[3]
    n_chunks = n_pages // cp
    n_total = n_seq * n_chunks
    past = n_pages * page
    t_new = knew_ref.shape[0]
    n_heads = qblk_ref.shape[0] // t_new
    int_min = jnp.int32(INT_MIN)

    def idx_pages(seq, action):
        def body(j, c):
            cpy = pltpu.make_async_copy(ckidx_ref.at[pt_ref[seq * n_pages + j]],
                                        kidx_buf.at[seq % 2, j], sem_i.at[seq % 2])
            cpy.start() if action == "start" else cpy.wait()
            return c

        lax.fori_loop(0, n_pages, body, 0)

    def kv_copies(g):
        return (_chunk_copies(ck_ref, kbuf, sem_k, pt_ref, g)
                + _chunk_copies(cv_ref, vbuf, sem_v, pt_ref, g))

    @pl.when(b == 0)
    def _():
        idx_pages(0, "start")

    _stream_prologue(kv_copies, n_total, lookahead)
    _pad_rows(knew_pad, knew_ref)
    _pad_rows(vnew_pad, vnew_ref)
    _pad_rows(kinew_pad, kinew_ref)
    idx_pages(b, "wait")

    @pl.when(b + 1 < n_seq)
    def _():
        idx_pages(b + 1, "start")

    qi2 = qi2_ref[...]
    wcol = wcol_ref[...]
    scale = (H_IDX * D_IDX) ** -0.5

    def head_sum(s):
        r = wcol * jnp.maximum(s, 0.0)
        acc = jnp.zeros((t_new, s.shape[1]), jnp.float32)
        for h in range(H_IDX):
            acc = acc + r[h * t_new:(h + 1) * t_new, :]
        return acc * scale

    pp = INDEXER_PAGES_PER_STEP

    def idx_body(c, carry):
        kk = jnp.concatenate([kidx_buf[b % 2, c * pp + p] for p in range(pp)], axis=1)
        sc = head_sum(_dot(qi2, kk.astype(MXU_DTYPE)))
        key_ref[:, pl.ds(pl.multiple_of(c * pp * page, pp * page), pp * page)] = _sortable_key(sc)
        return carry

    lax.fori_loop(0, n_pages // pp, idx_body, 0)
    sc_new = head_sum(_dot_nt(qi2, kinew_pad[...]))
    tok = lax.broadcasted_iota(jnp.int32, (t_new, LANES), 0)
    lane = lax.broadcasted_iota(jnp.int32, (t_new, LANES), 1)
    key_ref[:, past:past + kcs] = jnp.full((t_new, kcs), INT_MIN, jnp.int32)
    key_ref[:, past:past + LANES] = jnp.where(lane <= tok, _sortable_key(sc_new), int_min)
    thr = _select_topk(key_ref, cand_ref, (past + kcs) // kcs, kcs, k_sel, kt=min(kcs, DSA_KC))

    qblk = qblk_ref[...]

    def head_bias(key_cols):
        bias = jnp.where(key_cols >= thr, 0.0, NEG_INF)
        return jnp.concatenate([bias] * n_heads, axis=0)

    def kv_body(j, carry):
        m, l, acc = carry
        g = b * n_chunks + j
        _stream_wait(kv_copies, g, n_total, lookahead)
        kt = _chunk_transposed(kbuf, g % n_slots)
        vt = _chunk_transposed(vbuf, g % n_slots)
        cols = key_ref[:, pl.ds(pl.multiple_of(j * cp * page, cp * page), cp * page)]
        return _softmax_step(_dot(qblk, kt) + head_bias(cols), m, l, acc, vt,
                             v_is_transposed=True)

    rows = n_heads * t_new
    init = (jnp.full((rows, 1), M_INIT, jnp.float32), jnp.zeros((rows, 1), jnp.float32),
            jnp.zeros((rows, dd), jnp.float32))
    m, l, acc = lax.fori_loop(0, n_chunks, kv_body, init)
    s_new = _dot_nt(qblk, knew_pad[...]) + head_bias(key_ref[:, past:past + LANES])
    _, l, acc = _softmax_step(s_new, m, l, acc, vnew_pad[...])
    o_ref[...] = _heads_to_lanes(acc / l, n_heads, t_new)


def _block_diag_queries(qb, b, t, n_heads):
    q4 = qb.reshape(b, t, n_heads, HEAD_DIM)
    eye = jnp.eye(n_heads, dtype=qb.dtype)
    return jnp.einsum("bthd,hg->bhtgd", q4, eye).reshape(b, n_heads * t, n_heads * HEAD_DIM)


def _dsa_sample(qb, kb, vb, qib, kib2, wi, cache_k, cache_v, cache_kidx, page_table, b, t):
    dd = qb.shape[1]
    n_heads = dd // HEAD_DIM
    n_pool, page = cache_k.shape[0], cache_k.shape[1]
    n_pages = page_table.shape[1]
    past = n_pages * page
    k_sel = min(DSA_TOPK_MAX, (past + t) // 4)
    cp, n_slots = SAMPLE_PAGES_PER_CHUNK, SAMPLE_RING_SLOTS
    kcs = min(SAMPLE_SELECT_KC, past)
    assert n_pages % cp == 0 and past % kcs == 0 and t <= LANES and kcs >= k_sel
    assert n_pages % INDEXER_PAGES_PER_STEP == 0 and t == SUBLANES
    mx = MXU_DTYPE
    qblk = _block_diag_queries(qb, b, t, n_heads)
    qi2 = qib.reshape(b, t, H_IDX, D_IDX).transpose(0, 2, 1, 3).reshape(b, H_IDX * t, D_IDX)
    wcol = wi.reshape(b, t, H_IDX).transpose(0, 2, 1).reshape(b, H_IDX * t, 1)
    kinew = kib2[:, :D_IDX].reshape(b, t, D_IDX)
    per_b = lambda r, n: pl.BlockSpec((None, r, n), lambda bi, pt: (bi, 0, 0))
    anyspec = pl.BlockSpec(memory_space=pl.ANY)
    grid_spec = pltpu.PrefetchScalarGridSpec(
        num_scalar_prefetch=1, grid=(b,),
        in_specs=[per_b(n_heads * t, dd), per_b(H_IDX * t, D_IDX), per_b(H_IDX * t, 1),
                  per_b(t, dd), per_b(t, dd), per_b(t, D_IDX), anyspec, anyspec, anyspec],
        out_specs=per_b(t, dd),
        scratch_shapes=[
            pltpu.VMEM((2, n_pages, D_IDX, page), jnp.float32),
            pltpu.VMEM((t, past + kcs), jnp.int32),
            pltpu.VMEM((t, LANES), jnp.int32),
            pltpu.VMEM((n_slots, cp, n_heads, HEAD_DIM, page), jnp.float32),
            pltpu.VMEM((n_slots, cp, n_heads, HEAD_DIM, page), jnp.float32),
            pltpu.VMEM((LANES, dd), mx), pltpu.VMEM((LANES, dd), mx),
            pltpu.VMEM((LANES, D_IDX), mx),
            pltpu.SemaphoreType.DMA((2,)), pltpu.SemaphoreType.DMA((n_slots,)),
            pltpu.SemaphoreType.DMA((n_slots,)),
        ])
    out = pl.pallas_call(
        functools.partial(_dsa_sample_body, k_sel, b, n_pages, kcs),
        grid_spec=grid_spec,
        out_shape=jax.ShapeDtypeStruct((b, t, dd), jnp.float32),
        compiler_params=_cparams(("arbitrary",)), name="dsa_sample",
    )(page_table.reshape(-1), qblk, qi2, wcol, kb.reshape(b, t, dd), vb.reshape(b, t, dd),
      kinew, _pages_transposed(cache_kidx), _pages_transposed(cache_k),
      _pages_transposed(cache_v))
    return out.reshape(b * t, dd)


def _moba_sample_body(topk, n_seq, n_pages, pt_ref, qblk_ref, knew_ref, vnew_ref, ck_ref,
                      cv_ref, o_ref, s_ref, kbuf, vbuf, knew_pad, vnew_pad, sem_k, sem_v):
    b = pl.program_id(0)
    n_slots, cp = kbuf.shape[0], kbuf.shape[1]
    lookahead = n_slots - 1
    page = kbuf.shape[4]
    dm = kbuf.shape[2] * kbuf.shape[3]
    n_chunks = n_pages // cp
    n_total = n_seq * n_chunks
    ck = cp * page
    blk = MOBA_BLOCK
    bpc = ck // blk
    nb = n_pages * page // blk
    t_new = knew_ref.shape[0]
    n_heads = qblk_ref.shape[0] // t_new
    rows = n_heads * t_new

    k_copies = lambda g: _chunk_copies(ck_ref, kbuf, sem_k, pt_ref, g)
    v_copies = lambda g: _chunk_copies(cv_ref, vbuf, sem_v, pt_ref, g)
    _stream_prologue(k_copies, n_total, lookahead)
    _stream_prologue(v_copies, n_total, lookahead)
    _pad_rows(knew_pad, knew_ref)
    _pad_rows(vnew_pad, vnew_ref)
    qblk = qblk_ref[...]

    gate_lane = lax.broadcasted_iota(jnp.int32, (rows, LANES), 1)

    def k_body(j, gate):
        g_k = b * n_chunks + j
        _stream_wait(k_copies, g_k, n_total, lookahead)
        s = _dot(qblk, _chunk_transposed(kbuf, g_k % n_slots))
        s_ref[:, pl.ds(pl.multiple_of(j * ck, ck), ck)] = s
        for r in range(bpc):
            g = jnp.mean(s[:, r * blk:(r + 1) * blk], axis=1, keepdims=True)
            gate = jnp.where(gate_lane == j * bpc + r, g, gate)
        return gate

    gate = lax.fori_loop(0, n_chunks, k_body, jnp.zeros((rows, LANES), jnp.float32))
    sel = _moba_select(gate, 0, nb, topk).astype(MXU_DTYPE)
    n_row = lax.broadcasted_iota(jnp.int32, (LANES, ck), 0)
    col = lax.broadcasted_iota(jnp.int32, (LANES, ck), 1)
    blk_in_chunk = jnp.zeros((LANES, ck), jnp.int32)
    for r in range(1, bpc):
        blk_in_chunk = blk_in_chunk + jnp.where(col >= r * blk, 1, 0)

    def v_body(j, carry):
        m, l, acc = carry
        g_v = b * n_chunks + j
        _stream_wait(v_copies, g_v, n_total, lookahead)
        expand = jnp.where(n_row == j * bpc + blk_in_chunk, 1.0, 0.0).astype(MXU_DTYPE)
        s = jnp.where(_dot(sel, expand) > 0.5,
                      s_ref[:, pl.ds(pl.multiple_of(j * ck, ck), ck)], NEG_INF)
        return _softmax_step(s, m, l, acc, _chunk_transposed(vbuf, g_v % n_slots),
                             v_is_transposed=True)

    init = (jnp.full((rows, 1), M_INIT, jnp.float32), jnp.zeros((rows, 1), jnp.float32),
            jnp.zeros((rows, dm), jnp.float32))
    m, l, acc = lax.fori_loop(0, n_chunks, v_body, init)
    lane = lax.broadcasted_iota(jnp.int32, (rows, LANES), 1)
    s_new = jnp.where(lane <= _row_token(n_heads, t_new), _dot_nt(qblk, knew_pad[...]), NEG_INF)
    _, l, acc = _softmax_step(s_new, m, l, acc, vnew_pad[...])
    o_ref[...] = _heads_to_lanes(acc / l, n_heads, t_new)


def _moba_sample(qb, kb, vb, cache_k, cache_v, page_table, b, t):
    dm = qb.shape[1]
    n_heads = dm // HEAD_DIM
    n_pool, page = cache_k.shape[0], cache_k.shape[1]
    n_pages = page_table.shape[1]
    past = n_pages * page
    cp, n_slots = SAMPLE_PAGES_PER_CHUNK, SAMPLE_RING_SLOTS
    assert past % MOBA_BLOCK == 0 and t <= MOBA_BLOCK and t == SUBLANES
    assert n_pages % cp == 0 and (cp * page) % MOBA_BLOCK == 0
    nb = past // MOBA_BLOCK
    assert nb <= LANES
    topk = min(MOBA_TOPK, nb)
    mx = MXU_DTYPE
    qblk = _block_diag_queries(qb, b, t, n_heads)
    per_b = lambda r, n: pl.BlockSpec((None, r, n), lambda bi, pt: (bi, 0, 0))
    anyspec = pl.BlockSpec(memory_space=pl.ANY)
    grid_spec = pltpu.PrefetchScalarGridSpec(
        num_scalar_prefetch=1, grid=(b,),
        in_specs=[per_b(n_heads * t, dm), per_b(t, dm), per_b(t, dm), anyspec, anyspec],
        out_specs=per_b(t, dm),
        scratch_shapes=[
            pltpu.VMEM((n_heads * t, past), jnp.float32),
            pltpu.VMEM((n_slots, cp, n_heads, HEAD_DIM, page), jnp.float32),
            pltpu.VMEM((n_slots, cp, n_heads, HEAD_DIM, page), jnp.float32),
            pltpu.VMEM((LANES, dm), mx), pltpu.VMEM((LANES, dm), mx),
            pltpu.SemaphoreType.DMA((n_slots,)), pltpu.SemaphoreType.DMA((n_slots,)),
        ])
    out = pl.pallas_call(
        functools.partial(_moba_sample_body, topk, b, n_pages),
        grid_spec=grid_spec,
        out_shape=jax.ShapeDtypeStruct((b, t, dm), jnp.float32),
        compiler_params=_cparams(("arbitrary",)), name="moba_sample",
    )(page_table.reshape(-1), qblk, kb.reshape(b, t, dm), vb.reshape(b, t, dm),
      _pages_transposed(cache_k), _pages_transposed(cache_v))
    return out.reshape(b * t, dm)


def _trunk(x, p, conv_prev, ffn_prev, dsa_fn, moba_fn, w0_pad, w_conv_a0, wout0_a, wout0_b,
           w_in1, w_out1, ln_mix_g, ln_mix_b, ln_ffn_g, ln_ffn_b, w_ffn_in, w_conv_ffn,
           w_ffn_out, w_ple, w_ple_gate, alpha, c, dd, dm, is_prompt):
    b, t, d = x.shape
    m = b * t
    x2 = x.reshape(m, d)
    p2 = p.reshape(p.shape[0], m, p.shape[-1])

    ya, qb, k0, kb, v0, vb, qib, ki0, kib2, wi, conv_new = _proj0(
        x, w0_pad, w_conv_a0, conv_prev, c, dd)
    yb = dsa_fn(qb, kb, vb, qib, kib2, wi, b, t)
    x2 = _out_ln(x2, [ya, yb], [wout0_a, wout0_b], ln_mix_g[0], ln_mix_b[0], alpha)
    x2, st0 = _ffn(x2, p2[0], b, t, w_ffn_in[0], w_conv_ffn[0], w_ffn_out[0], ln_ffn_g[0],
                   ln_ffn_b[0], w_ple_gate[0], w_ple[0], ffn_prev[0], alpha)

    outs = _proj1(x2.reshape(b, t, d), w_in1, dm, is_prompt)
    qb1, k1, kb1, v1, vb1 = outs[:5]
    yc = moba_fn(qb1, kb1, vb1, *outs[5:], b, t)
    x2 = _out_ln(x2, [yc], [w_out1], ln_mix_g[1], ln_mix_b[1], alpha)
    x2, st1 = _ffn(x2, p2[1], b, t, w_ffn_in[1], w_conv_ffn[1], w_ffn_out[1], ln_ffn_g[1],
                   ln_ffn_b[1], w_ple_gate[1], w_ple[1], ffn_prev[1], alpha)

    h_dsa, h_moba = dd // HEAD_DIM, dm // HEAD_DIM
    return (x2.reshape(b, t, d), k0.reshape(b, t, h_dsa, HEAD_DIM),
            v0.reshape(b, t, h_dsa, HEAD_DIM), ki0.reshape(b, t, D_IDX), conv_new,
            k1.reshape(b, t, h_moba, HEAD_DIM), v1.reshape(b, t, h_moba, HEAD_DIM),
            jnp.stack([st0, st1]))


def kernel(x_prompt, x_sample, cache_k0, cache_v0, cache_kidx0, state_conv0, cache_k1, cache_v1, state_ffn, page_table, p_prompt, p_sample, w_in0, w_conv_a0, w_out0, w_in1, w_out1, ln_mix_g, ln_mix_b, ln_ffn_g, ln_ffn_b, w_ffn_in, w_conv_ffn, w_ffn_out, w_ple, w_ple_gate):
    depth = w_ffn_in.shape[0]
    assert depth == 2
    alpha = (2 * depth) ** 0.25
    d = x_prompt.shape[-1]
    c = w_conv_a0.shape[1]
    dd = cache_k0.shape[2] * HEAD_DIM
    dm = cache_k1.shape[2] * HEAD_DIM
    f = w_conv_ffn.shape[2]
    mx = MXU_DTYPE

    o = 3 * c + 3 * dd + H_IDX * D_IDX
    assert o % LANES == 0 and w_in0.shape[1] == o + D_IDX + H_IDX
    w0 = w_in0.astype(mx)
    ki_cols = w0[:, o:o + D_IDX]
    w0_pad = jnp.concatenate(
        [w0[:, :o], ki_cols, ki_cols, w0[:, o + D_IDX:], jnp.zeros((d, LANES - H_IDX), mx)],
        axis=1)
    wout0 = w_out0.astype(mx)
    shared = (w0_pad, w_conv_a0, wout0[:c], wout0[c:], w_in1.astype(mx), w_out1.astype(mx),
              ln_mix_g, ln_mix_b, ln_ffn_g, ln_ffn_b, w_ffn_in.astype(mx), w_conv_ffn,
              w_ffn_out.astype(mx), w_ple.astype(mx), w_ple_gate.astype(mx), alpha, c, dd, dm)

    bp = x_prompt.shape[0]
    conv_zero = jnp.zeros((bp, CONV_W - 1, c), jnp.float32)
    ffn_zero = jnp.zeros((depth, bp, CONV_W - 1, f), jnp.float32)
    prompt = _trunk(x_prompt, p_prompt, conv_zero, ffn_zero, _dsa_prompt, _moba_prompt,
                    *shared, True)

    def dsa_s(qb, kb, vb, qib, kib2, wi, b, t):
        return _dsa_sample(qb, kb, vb, qib, kib2, wi, cache_k0, cache_v0, cache_kidx0,
                           page_table, b, t)

    def moba_s(qb, kb, vb, b, t):
        return _moba_sample(qb, kb, vb, cache_k1, cache_v1, page_table, b, t)

    sample = _trunk(x_sample, p_sample, state_conv0, state_ffn, dsa_s, moba_s, *shared, False)
    y_p, k0_p, v0_p, ki_p, conv_p, k1_p, v1_p, ffn_p = prompt
    y_s, k0_s, v0_s, ki_s, conv_s, k1_s, v1_s, ffn_s = sample
    return (y_p, y_s, k0_p, v0_p, ki_p, conv_p, k1_p, v1_p, ffn_p,
            k0_s, v0_s, ki_s, conv_s, k1_s, v1_s, ffn_s)
```

```python
import functools

import jax
import jax.numpy as jnp
from jax import lax
from jax.experimental import pallas as pl
from jax.experimental.pallas import tpu as pltpu

HEAD_DIM = 64
H_IDX = 8
D_IDX = 64
DSA_TOPK_MAX = 256
MOBA_BLOCK = 256
MOBA_TOPK = 3
CONV_W = 3
LN_EPS = 1e-5

LANES = 128
SUBLANES = 8
VMEM_LIMIT_BYTES = 56 * 1024 * 1024
ROW_TILE = 256
DSA_TQ = 256
DSA_KC = 512
DSA_KA = 1024
MOBA_KC = 1024
SAMPLE_PAGES_PER_CHUNK = 4
SAMPLE_RING_SLOTS = 4
SAMPLE_SELECT_KC = 2048
INDEXER_PAGES_PER_STEP = 2

MXU_DTYPE = jnp.bfloat16
SELECT_ROW_BLOCK = 128
INT_MIN = -(2 ** 31)
NEG_INF = float("-inf")
M_INIT = -(2.0 ** 100)
MASK_BIAS = -(2.0 ** 101)


def _cparams(semantics):
    return pltpu.CompilerParams(dimension_semantics=semantics,
                                vmem_limit_bytes=VMEM_LIMIT_BYTES)


def _resident(shape):
    nd = len(shape)
    return pl.BlockSpec(shape, lambda *_: (0,) * nd, pipeline_mode=pl.Buffered(1))


def _dot(a, b):
    return jnp.dot(a, b, preferred_element_type=jnp.float32)


def _dot_nt(a, b):
    return lax.dot_general(a, b, (((1,), (1,)), ((), ())),
                           preferred_element_type=jnp.float32)


def _layer_norm(z, g, b):
    mu = jnp.mean(z, axis=-1, keepdims=True)
    var = jnp.mean(jnp.square(z - mu), axis=-1, keepdims=True)
    return (z - mu) * lax.rsqrt(var + LN_EPS) * g + b


def _conv3_carry(u, w_ref, ubuf_ref, prev_ref, first_tile):
    tm = u.shape[0]

    @pl.when(first_tile)
    def _():
        ubuf_ref[SUBLANES - 2:SUBLANES, :] = prev_ref[0]

    ubuf_ref[SUBLANES:SUBLANES + tm, :] = u
    u1 = ubuf_ref[SUBLANES - 1:SUBLANES - 1 + tm, :]
    u2 = ubuf_ref[SUBLANES - 2:SUBLANES - 2 + tm, :]
    y = u2 * w_ref[0:1, :] + u1 * w_ref[1:2, :] + u * w_ref[2:3, :]
    ubuf_ref[0:SUBLANES, :] = ubuf_ref[tm:tm + SUBLANES, :]
    return y


def _conv3_rows(u, w_ref, ubuf_ref, p1_ref, p2_ref, tpos_ref):
    tm = u.shape[0]
    ubuf_ref[0:SUBLANES, :] = jnp.zeros((SUBLANES, u.shape[1]), jnp.float32)
    ubuf_ref[SUBLANES:SUBLANES + tm, :] = u
    t = tpos_ref[...]
    u1 = jnp.where(t >= 1, ubuf_ref[SUBLANES - 1:SUBLANES - 1 + tm, :], p1_ref[...])
    u2 = jnp.where(t >= 2, ubuf_ref[SUBLANES - 2:SUBLANES - 2 + tm, :], p2_ref[...])
    return u2 * w_ref[0:1, :] + u1 * w_ref[1:2, :] + u * w_ref[2:3, :]


def _conv_prev_rows(prev, t_len):
    b, _, c = prev.shape
    z = jnp.zeros((b, t_len, c), jnp.float32)
    p1 = z.at[:, 0].set(prev[:, 1])
    p2 = z.at[:, 0].set(prev[:, 0]).at[:, 1].set(prev[:, 1])
    return p1.reshape(b * t_len, c), p2.reshape(b * t_len, c)


def _tpos(b, t_len):
    return jnp.tile(jnp.arange(t_len, dtype=jnp.int32), b).reshape(b * t_len, 1)


def _proj0_body(carry_mode, c, dd, x_ref, w_ref, wc_ref, *rest):
    if carry_mode:
        (prev_ref, ya_ref, qb_ref, k_ref, kb_ref, v_ref, vb_ref, qib_ref, ki_ref,
         kib2_ref, wi_ref, st_ref, ubuf_ref) = rest
    else:
        (p1_ref, p2_ref, tpos_ref, ya_ref, qb_ref, k_ref, kb_ref, v_ref, vb_ref,
         qib_ref, ki_ref, kib2_ref, wi_ref, u_ref, ubuf_ref) = rest
    res = _dot(x_ref[...].astype(MXU_DTYPE), w_ref[...])
    bg = res[:, 0:c]
    u = res[:, c:2 * c] * res[:, 2 * c:3 * c]
    if carry_mode:
        y = _conv3_carry(u, wc_ref, ubuf_ref, prev_ref, pl.program_id(1) == 0)
        st_ref[0] = u[u.shape[0] - 2:, :]
    else:
        y = _conv3_rows(u, wc_ref, ubuf_ref, p1_ref, p2_ref, tpos_ref)
        u_ref[...] = u
    ya_ref[...] = bg * y
    o = 3 * c
    q = res[:, o:o + dd]
    k = res[:, o + dd:o + 2 * dd]
    v = res[:, o + 2 * dd:o + 3 * dd]
    qb_ref[...] = (q * HEAD_DIM ** -0.5).astype(MXU_DTYPE)
    k_ref[...] = k
    kb_ref[...] = k.astype(MXU_DTYPE)
    v_ref[...] = v
    vb_ref[...] = v.astype(MXU_DTYPE)
    o += 3 * dd
    di = H_IDX * D_IDX
    qib_ref[...] = res[:, o:o + di].astype(MXU_DTYPE)
    o += di
    ki_ref[...] = res[:, o:o + D_IDX]
    kib2_ref[...] = res[:, o:o + 2 * D_IDX].astype(MXU_DTYPE)
    o += 2 * D_IDX
    wi_ref[...] = res[:, o:o + H_IDX]


def _proj0(x, w_pad, w_conv, prev, c, dd):
    b, t, d = x.shape
    m = b * t
    npad = w_pad.shape[1]
    di = H_IDX * D_IDX
    carry_mode = t % ROW_TILE == 0
    x2 = x.reshape(m, d)
    f32, mx = jnp.float32, MXU_DTYPE
    row_shapes = [(c, f32), (dd, mx), (dd, f32), (dd, mx), (dd, f32), (dd, mx),
                  (di, mx), (D_IDX, f32), (2 * D_IDX, mx), (H_IDX, f32)]
    if carry_mode:
        tm = ROW_TILE
        nt = t // tm
        grid = (b, nt)
        row = lambda n: pl.BlockSpec((tm, n), lambda bi, j: (bi * nt + j, 0))
        in_specs = [row(d), _resident((d, npad)), _resident((CONV_W, c)),
                    pl.BlockSpec((1, CONV_W - 1, c), lambda bi, j: (bi, 0, 0))]
        out_specs = [row(n) for n, _ in row_shapes] + [
            pl.BlockSpec((1, CONV_W - 1, c), lambda bi, j: (bi, 0, 0))]
        out_shape = [jax.ShapeDtypeStruct((m, n), dt) for n, dt in row_shapes] + [
            jax.ShapeDtypeStruct((b, CONV_W - 1, c), f32)]
        args = (x2, w_pad, w_conv, prev)
        sem = ("arbitrary", "arbitrary")
    else:
        assert m % SUBLANES == 0 and t >= CONV_W - 1
        tm = m
        grid = (1,)
        row = lambda n: pl.BlockSpec((tm, n), lambda i: (0, 0))
        p1, p2 = _conv_prev_rows(prev, t)
        in_specs = [row(d), _resident((d, npad)), _resident((CONV_W, c)),
                    row(c), row(c), row(1)]
        out_specs = [row(n) for n, _ in row_shapes] + [row(c)]
        out_shape = [jax.ShapeDtypeStruct((m, n), dt) for n, dt in row_shapes] + [
            jax.ShapeDtypeStruct((m, c), f32)]
        args = (x2, w_pad, w_conv, p1, p2, _tpos(b, t))
        sem = ("arbitrary",)
    outs = pl.pallas_call(
        functools.partial(_proj0_body, carry_mode, c, dd),
        grid=grid, in_specs=in_specs, out_specs=out_specs, out_shape=out_shape,
        scratch_shapes=[pltpu.VMEM((tm + 2 * SUBLANES, c), f32)],
        compiler_params=_cparams(sem), name="proj0",
    )(*args)
    outs = list(outs)
    if not carry_mode:
        outs[-1] = outs[-1].reshape(b, t, c)[:, t - (CONV_W - 1):, :]
    return outs


def _proj1_body(blocks_per_seq, dm, x_ref, w_ref, qb_ref, k_ref, ka_ref, v_ref, va_ref, *rest):
    res = _dot(x_ref[...].astype(MXU_DTYPE), w_ref[...])
    k = res[:, dm:2 * dm]
    v = res[:, 2 * dm:3 * dm]
    qb_ref[...] = (res[:, 0:dm] * HEAD_DIM ** -0.5).astype(MXU_DTYPE)
    k_ref[...] = k
    v_ref[...] = v
    if blocks_per_seq is None:
        ka_ref[...] = k.astype(MXU_DTYPE)
        va_ref[...] = v.astype(MXU_DTYPE)
        return
    rest[0][0] = jnp.mean(k, axis=0, keepdims=True)
    blk = pl.program_id(0) % blocks_per_seq
    lane = lax.broadcasted_iota(jnp.int32, (k.shape[0], LANES), 1)
    low = lane < HEAD_DIM
    e_high = jnp.where(lane == HEAD_DIM + blk, MASK_BIAS, 0.0)
    e_low = jnp.where(lane == blk, MASK_BIAS, 0.0)
    for hp in range(dm // LANES):
        kp = k[:, hp * LANES:(hp + 1) * LANES]
        vp = v[:, hp * LANES:(hp + 1) * LANES]
        even = slice(2 * hp * LANES, (2 * hp + 1) * LANES)
        odd = slice((2 * hp + 1) * LANES, (2 * hp + 2) * LANES)
        ka_ref[:, even] = jnp.where(low, kp, e_high).astype(MXU_DTYPE)
        ka_ref[:, odd] = jnp.where(low, e_low, kp).astype(MXU_DTYPE)
        va_ref[:, even] = jnp.where(low, vp, 1.0).astype(MXU_DTYPE)
        va_ref[:, odd] = jnp.where(low, 1.0, vp).astype(MXU_DTYPE)


def _proj1(x, w, dm, is_prompt):
    b, t, d = x.shape
    m = b * t
    f32, mx = jnp.float32, MXU_DTYPE
    if is_prompt:
        assert t % MOBA_BLOCK == 0 and t // MOBA_BLOCK <= HEAD_DIM
        tm = MOBA_BLOCK
        da = 2 * dm
        blocks_per_seq = t // MOBA_BLOCK
    else:
        tm = m
        da = dm
        blocks_per_seq = None
    nt = m // tm
    row = lambda n: pl.BlockSpec((tm, n), lambda i: (i, 0))
    out_specs = [row(dm), row(dm), row(da), row(dm), row(da)]
    out_shape = [jax.ShapeDtypeStruct((m, n), dt)
                 for n, dt in ((dm, mx), (dm, f32), (da, mx), (dm, f32), (da, mx))]
    if is_prompt:
        out_specs.append(pl.BlockSpec((1, 1, dm), lambda i: (i, 0, 0)))
        out_shape.append(jax.ShapeDtypeStruct((nt, 1, dm), f32))
    return pl.pallas_call(
        functools.partial(_proj1_body, blocks_per_seq, dm),
        grid=(nt,), in_specs=[row(d), _resident((d, 3 * dm))],
        out_specs=out_specs, out_shape=out_shape,
        compiler_params=_cparams(("arbitrary",)), name="proj1",
    )(x.reshape(m, d), w)


def _out_ln_body(n_act, alpha, x_ref, *rest):
    acts = rest[:n_act]
    ws = rest[n_act:2 * n_act]
    g_ref, b_ref, o_ref = rest[2 * n_act:]
    y = _dot(acts[0][...].astype(MXU_DTYPE), ws[0][...])
    for a_ref, w_ref in zip(acts[1:], ws[1:]):
        y = y + _dot(a_ref[...].astype(MXU_DTYPE), w_ref[...])
    o_ref[...] = _layer_norm(alpha * x_ref[...] + y, g_ref[...], b_ref[...])


def _out_ln(x2, acts, ws, g, bta, alpha):
    m, d = x2.shape
    tm = ROW_TILE if m % ROW_TILE == 0 else m
    row = lambda n: pl.BlockSpec((tm, n), lambda i: (i, 0))
    in_specs = ([row(d)] + [row(a.shape[1]) for a in acts]
                + [_resident(w.shape) for w in ws] + [_resident((1, d))] * 2)
    return pl.pallas_call(
        functools.partial(_out_ln_body, len(acts), alpha),
        grid=(m // tm,), in_specs=in_specs, out_specs=row(d),
        out_shape=jax.ShapeDtypeStruct((m, d), jnp.float32),
        compiler_params=_cparams(("arbitrary",)), name="out_ln",
    )(x2, *acts, *ws, g.reshape(1, d), bta.reshape(1, d))


def _ffn_body(carry_mode, f, alpha, x_ref, p_ref, win_ref, wc_ref, wout_ref, g_ref, b_ref,
              wg_ref, wp_ref, *rest):
    if carry_mode:
        prev_ref, o_ref, st_ref, ubuf_ref = rest
    else:
        p1_ref, p2_ref, tpos_ref, o_ref, u_ref, ubuf_ref = rest
    x = x_ref[...]
    h = _dot(x.astype(MXU_DTYPE), win_ref[...])
    u = h[:, 0:f]
    if carry_mode:
        uc = _conv3_carry(u, wc_ref, ubuf_ref, prev_ref, pl.program_id(1) == 0)
        st_ref[0] = u[u.shape[0] - 2:, :]
    else:
        uc = _conv3_rows(u, wc_ref, ubuf_ref, p1_ref, p2_ref, tpos_ref)
        u_ref[...] = u
    a = jax.nn.gelu(uc) * h[:, f:2 * f]
    y = _dot(a.astype(MXU_DTYPE), wout_ref[...])
    x2 = _layer_norm(alpha * x + y, g_ref[...], b_ref[...])
    gate = jax.nn.sigmoid(_dot(x2.astype(MXU_DTYPE), wg_ref[...]))
    o_ref[...] = x2 + gate * _dot(p_ref[...].astype(MXU_DTYPE), wp_ref[...])


def _ffn(x2, p2d, b, t, win, wc, wout, g, bta, wg, wp, prev, alpha):
    m, d = x2.shape
    f = wc.shape[1]
    dp = p2d.shape[1]
    carry_mode = t % ROW_TILE == 0
    f32 = jnp.float32
    weights = [_resident((d, 2 * f)), _resident((CONV_W, f)), _resident((f, d)),
               _resident((1, d)), _resident((1, d)), _resident((d, d)), _resident((dp, d))]
    if carry_mode:
        tm = ROW_TILE
        nt = t // tm
        grid = (b, nt)
        row = lambda n: pl.BlockSpec((tm, n), lambda bi, j: (bi * nt + j, 0))
        st_spec = pl.BlockSpec((1, CONV_W - 1, f), lambda bi, j: (bi, 0, 0))
        in_specs = [row(d), row(dp)] + weights + [st_spec]
        out_specs = [row(d), st_spec]
        out_shape = [jax.ShapeDtypeStruct((m, d), f32),
                     jax.ShapeDtypeStruct((b, CONV_W - 1, f), f32)]
        extra = (prev,)
        sem = ("arbitrary", "arbitrary")
    else:
        tm = m
        grid = (1,)
        row = lambda n: pl.BlockSpec((tm, n), lambda i: (0, 0))
        p1, p2 = _conv_prev_rows(prev, t)
        in_specs = [row(d), row(dp)] + weights + [row(f), row(f), row(1)]
        out_specs = [row(d), row(f)]
        out_shape = [jax.ShapeDtypeStruct((m, d), f32), jax.ShapeDtypeStruct((m, f), f32)]
        extra = (p1, p2, _tpos(b, t))
        sem = ("arbitrary",)
    xo, st = pl.pallas_call(
        functools.partial(_ffn_body, carry_mode, f, alpha),
        grid=grid, in_specs=in_specs, out_specs=out_specs, out_shape=out_shape,
        scratch_shapes=[pltpu.VMEM((tm + 2 * SUBLANES, f), f32)],
        compiler_params=_cparams(sem), name="ffn",
    )(x2, p2d, win, wc, wout, g.reshape(1, d), bta.reshape(1, d), wg, wp, *extra)
    if not carry_mode:
        st = st.reshape(b, t, f)[:, t - (CONV_W - 1):, :]
    return xo, st


def _sortable_key(sc):
    bits = pltpu.bitcast(sc, jnp.int32)
    return bits ^ ((bits >> 31) & 0x7FFFFFFF)


def _fold_lanes(x):
    part = x[:, 0:LANES]
    for j in range(1, x.shape[1] // LANES):
        part = part + x[:, j * LANES:(j + 1) * LANES]
    return part


def _row_blocks(rows):
    rb = min(rows, SELECT_ROW_BLOCK)
    assert rows % rb == 0
    return [(r0, rb) for r0 in range(0, rows, rb)]


def _count_ge(key_ref, cand_ref, cand, n_chunks, kc):
    blocks = _row_blocks(key_ref.shape[0])
    cand_ref[...] = jnp.broadcast_to(cand, cand_ref.shape)

    def body(c, accs):
        out = []
        for (r0, rb), acc in zip(blocks, accs):
            cr = cand_ref[r0:r0 + rb, :]
            kk = key_ref[r0:r0 + rb, pl.ds(pl.multiple_of(c * kc, kc), kc)]
            for j in range(kc // LANES):
                acc = acc + jnp.where(kk[:, j * LANES:(j + 1) * LANES] >= cr, 1.0, 0.0)
            out.append(acc)
        return tuple(out)

    init = tuple(jnp.zeros((rb, LANES), jnp.float32) for _, rb in blocks)
    accs = lax.fori_loop(0, n_chunks, body, init)
    acc = accs[0] if len(accs) == 1 else jnp.concatenate(accs, axis=0)
    return jnp.sum(acc, axis=1, keepdims=True)


def _select_topk(key_ref, cand_ref, n_chunks, kc, k_sel, kt=None):
    rows = key_ref.shape[0]
    int_min = jnp.int32(INT_MIN)
    kt = kc if kt is None else kt
    assert kc % kt == 0

    def bit_body(i, t_u):
        cand_u = t_u | lax.shift_left(jnp.int32(1), 31 - i)
        cnt = _count_ge(key_ref, cand_ref, cand_u ^ int_min, n_chunks, kc)
        return jnp.where(cnt >= k_sel, cand_u, t_u)

    t_u = lax.fori_loop(0, 32, bit_body, jnp.zeros((rows, 1), jnp.int32))
    thr = jnp.maximum(t_u ^ int_min, int_min + 1)
    cnt_ge = _count_ge(key_ref, cand_ref, thr, n_chunks, kc)
    excess = jnp.max(jnp.where(cnt_ge > k_sel, 1.0, 0.0))

    @pl.when(excess > 0.0)
    def _():
        cnt_gt = _count_ge(key_ref, cand_ref, thr + 1, n_chunks, kc)
        need = k_sel - cnt_gt
        r_i = lax.broadcasted_iota(jnp.int32, (kt, kt), 0)
        c_i = lax.broadcasted_iota(jnp.int32, (kt, kt), 1)
        tri = jnp.where(r_i <= c_i, 1.0, 0.0).astype(MXU_DTYPE)

        def body(c, seen):
            sl = pl.ds(pl.multiple_of(c * kt, kt), kt)
            kk = key_ref[:, sl]
            tie = kk == thr
            tie_f = jnp.where(tie, 1.0, 0.0)
            rank = _dot(tie_f.astype(MXU_DTYPE), tri) + seen
            key_ref[:, sl] = jnp.where(tie & (rank > need), int_min, kk)
            return seen + jnp.sum(tie_f, axis=1, keepdims=True)

        lax.fori_loop(0, n_chunks * (kc // kt), body, jnp.zeros((rows, 1), jnp.float32))

    return thr


def _softmax_step(s, m, l, acc, v, v_is_transposed=False):
    m_new = jnp.maximum(m, jnp.max(s, axis=1, keepdims=True))
    a = jnp.exp(m - m_new)
    p = jnp.exp(s - m_new)
    l_new = a * l + jnp.sum(p, axis=1, keepdims=True)
    pv = (_dot_nt if v_is_transposed else _dot)(p.astype(MXU_DTYPE), v)
    return m_new, l_new, a * acc + pv


def _head_halves(x):
    lane = lax.broadcasted_iota(jnp.int32, x.shape, 1)
    zero = jnp.zeros_like(x)
    return jnp.where(lane < HEAD_DIM, x, zero), jnp.where(lane >= HEAD_DIM, x, zero)


def _dsa_prompt_body(k_sel, qb_ref, kb_ref, vb_ref, qib_ref, kib2_ref, wi_ref, o_ref,
                     key_ref, cand_ref, thr_ref, m_ref, acc_ref, sa_ref, sb_ref):
    i = pl.program_id(1)
    hp = pl.program_id(2)
    tq = qb_ref.shape[0]
    kc = DSA_KC
    ka = min(DSA_KA, kb_ref.shape[0])
    n_att = ((i + 1) * tq + ka - 1) // ka
    n_chunks = n_att * (ka // kc)
    qpos = i * tq + lax.broadcasted_iota(jnp.int32, (tq, 1), 0)

    @pl.when(hp == 0)
    def _():
        qi = qib_ref[...]
        w = wi_ref[...]
        qm = []
        for hh in range(H_IDX // 2):
            qm.extend(_head_halves(qi[:, hh * LANES:(hh + 1) * LANES]))
        qstack = jnp.concatenate(qm, axis=0)
        wcol = [w[:, h:h + 1] for h in range(H_IDX)]

        def idx_body(c, carry):
            sl = pl.ds(pl.multiple_of(c * kc, kc), kc)
            s_all = _dot_nt(qstack, kib2_ref[sl, :])
            acc = jnp.zeros((tq, kc), jnp.float32)
            for h in range(H_IDX):
                acc = acc + wcol[h] * jnp.maximum(s_all[h * tq:(h + 1) * tq, :], 0.0)
            sc = acc * (H_IDX * D_IDX) ** -0.5
            kpos = c * kc + lax.broadcasted_iota(jnp.int32, (1, kc), 1)
            key_ref[:, sl] = jnp.where(kpos <= qpos, _sortable_key(sc), jnp.int32(INT_MIN))
            return carry

        lax.fori_loop(0, n_chunks, idx_body, 0)
        thr_ref[...] = _select_topk(key_ref, cand_ref, n_att, ka, k_sel, kt=kc)

    thr = thr_ref[...]
    qs = _head_halves(qb_ref[...])
    m_ref[...] = jnp.full(m_ref.shape, M_INIT, jnp.float32)
    acc_ref[...] = jnp.zeros(acc_ref.shape, jnp.float32)

    def scores_into(s_ref, c):
        rows = pl.ds(pl.multiple_of(c * ka, ka), ka)
        kp = kb_ref[rows, :]
        sel = key_ref[:, rows] >= thr
        for h in range(2):
            s_ref[h] = jnp.where(sel, _dot_nt(qs[h], kp), MASK_BIAS)

    ones = jnp.ones((ka, LANES), MXU_DTYPE)

    def consume(s_ref, c):
        vp1 = jnp.concatenate([vb_ref[pl.ds(pl.multiple_of(c * ka, ka), ka), :], ones], axis=1)
        for h in range(2):
            s = s_ref[h]
            m_old = m_ref[h]
            m_new = jnp.maximum(m_old, jnp.max(s, axis=1, keepdims=True))
            p = jnp.exp(s - m_new).astype(MXU_DTYPE)
            acc_ref[h] = jnp.exp(m_old - m_new) * acc_ref[h] + _dot(p, vp1)
            m_ref[h] = m_new

    _two_stage_loop(n_att, scores_into, consume, sa_ref, sb_ref)
    lane = lax.broadcasted_iota(jnp.int32, (tq, LANES), 1)
    a0, a1 = acc_ref[0], acc_ref[1]
    o_ref[...] = jnp.where(lane < HEAD_DIM, a0[:, :LANES] / a0[:, LANES:],
                           a1[:, :LANES] / a1[:, LANES:])


def _dsa_prompt(qb, kb, vb, qib, kib2, wi, b, t):
    dd = qb.shape[1]
    k_sel = min(DSA_TOPK_MAX, t // 4)
    tq, kc, ka = DSA_TQ, DSA_KC, min(DSA_KA, t)
    assert t % ka == 0 and ka % kc == 0 and ka % tq == 0 and kc >= k_sel and dd % LANES == 0
    nq, n_hp = t // tq, dd // LANES
    r3 = lambda a: a.reshape(b, t, a.shape[1])
    di = qib.shape[1]
    f32 = jnp.float32
    out = pl.pallas_call(
        functools.partial(_dsa_prompt_body, k_sel),
        grid=(b, nq, n_hp),
        in_specs=[
            pl.BlockSpec((None, tq, LANES), lambda bi, i, hp: (bi, i, hp)),
            pl.BlockSpec((None, t, LANES), lambda bi, i, hp: (bi, 0, hp)),
            pl.BlockSpec((None, t, LANES), lambda bi, i, hp: (bi, 0, hp)),
            pl.BlockSpec((None, tq, di), lambda bi, i, hp: (bi, i, 0)),
            pl.BlockSpec((None, t, 2 * D_IDX), lambda bi, i, hp: (bi, 0, 0),
                         pipeline_mode=pl.Buffered(1)),
            pl.BlockSpec((None, tq, H_IDX), lambda bi, i, hp: (bi, i, 0)),
        ],
        out_specs=pl.BlockSpec((None, tq, LANES), lambda bi, i, hp: (bi, i, hp)),
        out_shape=jax.ShapeDtypeStruct((b, t, dd), f32),
        scratch_shapes=[pltpu.VMEM((tq, t), jnp.int32), pltpu.VMEM((tq, LANES), jnp.int32),
                        pltpu.VMEM((tq, 1), jnp.int32),
                        pltpu.VMEM((2, tq, 1), f32), pltpu.VMEM((2, tq, 2 * LANES), f32),
                        pltpu.VMEM((2, tq, ka), f32), pltpu.VMEM((2, tq, ka), f32)],
        compiler_params=_cparams(("arbitrary", "arbitrary", "arbitrary")),
        name="dsa_prompt",
    )(r3(qb), r3(kb), r3(vb), r3(qib), r3(kib2), r3(wi))
    return out.reshape(b * t, dd)


def _moba_select(gate, first_lane, n_valid, topk):
    rows, n = gate.shape
    n_idx = lax.broadcasted_iota(jnp.int32, (rows, n), 1)
    n_f = n_idx.astype(jnp.float32)

    def in_window(x, other):
        return jnp.where(n_idx >= first_lane,
                         jnp.where(n_idx < first_lane + n_valid, x, other), other)

    g = in_window(gate, NEG_INF)
    picked = jnp.zeros((rows, n), jnp.float32)
    for _ in range(topk):
        mx = jnp.max(g, axis=1, keepdims=True)
        first = jnp.min(jnp.where(g == mx, n_f, float(n)), axis=1, keepdims=True)
        hit = n_f == first
        picked = jnp.where(hit, 1.0, picked)
        g = jnp.where(hit, NEG_INF, g)
    return in_window(picked, 0.0)


def _two_stage_loop(n, produce, consume, buf_a, buf_b):
    last = jnp.maximum(n - 1, 0)

    @pl.when(n > 0)
    def _():
        produce(buf_a, 0)

    def pair(p, carry):
        c0 = 2 * p
        produce(buf_b, jnp.minimum(c0 + 1, last))
        consume(buf_a, c0)

        @pl.when(c0 + 1 < n)
        def _():
            produce(buf_a, jnp.minimum(c0 + 2, last))
            consume(buf_b, c0 + 1)

        return carry

    lax.fori_loop(0, (n + 1) // 2, pair, 0)


def _moba_prompt_body(topk, qb_ref, ka_ref, va_ref, kmx_ref, o_ref, m_ref, acc_ref,
                      sa_ref, sb_ref):
    i = pl.program_id(2)
    tq = qb_ref.shape[0]
    blk = MOBA_BLOCK
    kc = min(MOBA_KC, ka_ref.shape[0])
    q2 = qb_ref[...]
    lane = lax.broadcasted_iota(jnp.int32, (tq, LANES), 1)
    low = lane < HEAD_DIM
    q0, q1 = _head_halves(q2)
    kmx = kmx_ref[...].astype(MXU_DTYPE)
    pick0 = _moba_select(_dot_nt(q0, kmx[0:LANES]), HEAD_DIM, i, topk)
    pick1 = _moba_select(_dot_nt(q1, kmx[LANES:2 * LANES]), 0, i, topk)
    zero = jnp.zeros_like(q2)
    qa = (jnp.where(low, q2, (1.0 - pick0).astype(MXU_DTYPE)),
          jnp.where(low, (1.0 - pick1).astype(MXU_DTYPE), q2))
    m_ref[...] = jnp.full(m_ref.shape, M_INIT, jnp.float32)
    acc_ref[...] = jnp.zeros(acc_ref.shape, jnp.float32)

    def update(h, s, v):
        m_old = m_ref[h]
        m_new = jnp.maximum(m_old, jnp.max(s, axis=1, keepdims=True))
        p = jnp.exp(s - m_new).astype(MXU_DTYPE)
        acc_ref[h] = jnp.exp(m_old - m_new) * acc_ref[h] + _dot(p, v)
        m_ref[h] = m_new

    def scores_into(s_ref, c):
        rows = pl.ds(pl.multiple_of(c * kc, kc), kc)
        for h in range(2):
            s_ref[h] = _dot_nt(qa[h], ka_ref[rows, h * LANES:(h + 1) * LANES])

    def consume(s_ref, c):
        rows = pl.ds(pl.multiple_of(c * kc, kc), kc)
        for h in range(2):
            update(h, s_ref[h], va_ref[rows, h * LANES:(h + 1) * LANES])

    _two_stage_loop((i * blk + kc - 1) // kc, scores_into, consume, sa_ref, sb_ref)

    rows = pl.ds(pl.multiple_of(i * blk, blk), blk)
    causal = (lax.broadcasted_iota(jnp.int32, (tq, blk), 1)
              <= lax.broadcasted_iota(jnp.int32, (tq, blk), 0))
    own_lane = (HEAD_DIM + i, i)
    for h in range(2):
        cols = slice(h * LANES, (h + 1) * LANES)
        q_own = jnp.where(lane == own_lane[h], zero, qa[h])
        s = jnp.where(causal, _dot_nt(q_own, ka_ref[rows, cols]), MASK_BIAS)
        update(h, s, va_ref[rows, cols])
    a0, a1 = acc_ref[0], acc_ref[1]
    o_ref[...] = jnp.where(low, a0 / pltpu.roll(a0, HEAD_DIM, 1), a1 / pltpu.roll(a1, HEAD_DIM, 1))


def _moba_prompt(qb, ka, va, kmean, b, t):
    dm = qb.shape[1]
    blk = MOBA_BLOCK
    nb = t // blk
    n_hp = dm // LANES
    assert t % blk == 0 and dm % LANES == 0 and nb <= HEAD_DIM and t % min(MOBA_KC, t) == 0
    topk = min(MOBA_TOPK, nb - 1)
    kmp = kmean.reshape(b, nb, n_hp, LANES).transpose(0, 2, 1, 3)
    kmp = jnp.pad(kmp, ((0, 0), (0, 0), (0, HEAD_DIM - nb), (0, 0)))
    z = jnp.zeros_like(kmp)
    kmx = jnp.concatenate([z, kmp, kmp, z], axis=2)
    resident = lambda n: pl.BlockSpec((None, t, n), lambda bi, hp, i: (bi, 0, hp),
                                      pipeline_mode=pl.Buffered(1))
    out = pl.pallas_call(
        functools.partial(_moba_prompt_body, topk),
        grid=(b, n_hp, nb),
        in_specs=[
            pl.BlockSpec((None, blk, LANES), lambda bi, hp, i: (bi, i, hp)),
            resident(2 * LANES), resident(2 * LANES),
            pl.BlockSpec((None, None, 2 * LANES, LANES), lambda bi, hp, i: (bi, hp, 0, 0)),
        ],
        out_specs=pl.BlockSpec((None, blk, LANES), lambda bi, hp, i: (bi, i, hp)),
        out_shape=jax.ShapeDtypeStruct((b, t, dm), jnp.float32),
        scratch_shapes=[pltpu.VMEM((2, blk, 1), jnp.float32),
                        pltpu.VMEM((2, blk, LANES), jnp.float32),
                        pltpu.VMEM((2, blk, min(MOBA_KC, t)), jnp.float32),
                        pltpu.VMEM((2, blk, min(MOBA_KC, t)), jnp.float32)],
        compiler_params=_cparams(("arbitrary", "arbitrary", "arbitrary")),
        name="moba_prompt",
    )(qb.reshape(b, t, dm), ka.reshape(b, t, 2 * dm), va.reshape(b, t, 2 * dm), kmx)
    return out.reshape(b * t, dm)


def _chunk_copies(cache_ref, buf_ref, sem, pt_ref, g):
    n_slots, cp = buf_ref.shape[0], buf_ref.shape[1]
    slot = g % n_slots
    return [pltpu.make_async_copy(cache_ref.at[pt_ref[g * cp + p]],
                                  buf_ref.at[slot, p], sem.at[slot])
            for p in range(cp)]


def _stream_prologue(copies, n_total, lookahead):
    assert n_total >= lookahead

    @pl.when(pl.program_id(0) == 0)
    def _():
        for g in range(lookahead):
            for cpy in copies(g):
                cpy.start()


def _stream_wait(copies, g, n_total, lookahead):
    for cpy in copies(g):
        cpy.wait()

    @pl.when(g + lookahead < n_total)
    def _():
        for cpy in copies(g + lookahead):
            cpy.start()


def _chunk_transposed(buf_ref, slot):
    cp, nh, dh, page = buf_ref.shape[1:]
    return jnp.concatenate([buf_ref[slot, p].reshape(nh * dh, page) for p in range(cp)],
                           axis=1).astype(MXU_DTYPE)


def _pages_transposed(cache):
    nd = cache.ndim
    return jnp.transpose(cache, (0,) + tuple(range(2, nd)) + (1,))


def _pad_rows(dst_ref, src_ref):
    dst_ref[...] = jnp.zeros(dst_ref.shape, dst_ref.dtype)
    dst_ref[0:src_ref.shape[0], :] = src_ref[...]


def _row_token(n_heads, t_new):
    return lax.broadcasted_iota(jnp.int32, (n_heads, t_new, LANES), 1).reshape(
        n_heads * t_new, LANES)


def _heads_to_lanes(x, n_heads, t_new):
    lane_head = lax.broadcasted_iota(jnp.int32, (t_new, x.shape[1]), 1) // HEAD_DIM
    out = jnp.zeros((t_new, x.shape[1]), jnp.float32)
    for h in range(n_heads):
        out = jnp.where(lane_head == h, x[h * t_new:(h + 1) * t_new, :], out)
    return out


def _dsa_sample_body(k_sel, n_seq, n_pages, kcs, pt_ref, qblk_ref, qi2_ref, wcol_ref,
                     knew_ref, vnew_ref, kinew_ref, ckidx_ref, ck_ref, cv_ref, o_ref,
                     kidx_buf, key_ref, cand_ref, kbuf, vbuf, knew_pad, vnew_pad, kinew_pad,
                     sem_i, sem_k, sem_v):
    b = pl.program_id(0)
    n_slots, cp = kbuf.shape[0], kbuf.shape[1]
    lookahead = n_slots - 1
    page = kbuf.shape[4]
    dd = kbuf.shape[2] * kbuf.shape[3]
    n_chunks = n_pages // cp
    n_total = n_seq * n_chunks
    past = n_pages * page
    t_new = knew_ref.shape[0]
    n_heads = qblk_ref.shape[0] // t_new
    int_min = jnp.int32(INT_MIN)

    def idx_pages(seq, action):
        def body(j, c):
            cpy = pltpu.make_async_copy(ckidx_ref.at[pt_ref[seq * n_pages + j]],
                                        kidx_buf.at[seq % 2, j], sem_i.at[seq % 2])
            cpy.start() if action == "start" else cpy.wait()
            return c

        lax.fori_loop(0, n_pages, body, 0)

    def kv_copies(g):
        return (_chunk_copies(ck_ref, kbuf, sem_k, pt_ref, g)
                + _chunk_copies(cv_ref, vbuf, sem_v, pt_ref, g))

    @pl.when(b == 0)
    def _():
        idx_pages(0, "start")

    _stream_prologue(kv_copies, n_total, lookahead)
    _pad_rows(knew_pad, knew_ref)
    _pad_rows(vnew_pad, vnew_ref)
    _pad_rows(kinew_pad, kinew_ref)
    idx_pages(b, "wait")

    @pl.when(b + 1 < n_seq)
    def _():
        idx_pages(b + 1, "start")

    qi2 = qi2_ref[...]
    wcol = wcol_ref[...]
    scale = (H_IDX * D_IDX) ** -0.5

    def head_sum(s):
        r = wcol * jnp.maximum(s, 0.0)
        acc = jnp.zeros((t_new, s.shape[1]), jnp.float32)
        for h in range(H_IDX):
            acc = acc + r[h * t_new:(h + 1) * t_new, :]
        return acc * scale

    pp = INDEXER_PAGES_PER_STEP

    def idx_body(c, carry):
        kk = jnp.concatenate([kidx_buf[b % 2, c * pp + p] for p in range(pp)], axis=1)
        sc = head_sum(_dot(qi2, kk.astype(MXU_DTYPE)))
        key_ref[:, pl.ds(pl.multiple_of(c * pp * page, pp * page), pp * page)] = _sortable_key(sc)
        return carry

    lax.fori_loop(0, n_pages // pp, idx_body, 0)
    sc_new = head_sum(_dot_nt(qi2, kinew_pad[...]))
    tok = lax.broadcasted_iota(jnp.int32, (t_new, LANES), 0)
    lane = lax.broadcasted_iota(jnp.int32, (t_new, LANES), 1)
    key_ref[:, past:past + kcs] = jnp.full((t_new, kcs), INT_MIN, jnp.int32)
    key_ref[:, past:past + LANES] = jnp.where(lane <= tok, _sortable_key(sc_new), int_min)
    thr = _select_topk(key_ref, cand_ref, (past + kcs) // kcs, kcs, k_sel, kt=min(kcs, DSA_KC))

    qblk = qblk_ref[...]

    def head_bias(key_cols):
        bias = jnp.where(key_cols >= thr, 0.0, NEG_INF)
        return jnp.concatenate([bias] * n_heads, axis=0)

    def kv_body(j, carry):
        m, l, acc = carry
        g = b * n_chunks + j
        _stream_wait(kv_copies, g, n_total, lookahead)
        kt = _chunk_transposed(kbuf, g % n_slots)
        vt = _chunk_transposed(vbuf, g % n_slots)
        cols = key_ref[:, pl.ds(pl.multiple_of(j * cp * page, cp * page), cp * page)]
        return _softmax_step(_dot(qblk, kt) + head_bias(cols), m, l, acc, vt,
                             v_is_transposed=True)

    rows = n_heads * t_new
    init = (jnp.full((rows, 1), M_INIT, jnp.float32), jnp.zeros((rows, 1), jnp.float32),
            jnp.zeros((rows, dd), jnp.float32))
    m, l, acc = lax.fori_loop(0, n_chunks, kv_body, init)
    s_new = _dot_nt(qblk, knew_pad[...]) + head_bias(key_ref[:, past:past + LANES])
    _, l, acc = _softmax_step(s_new, m, l, acc, vnew_pad[...])
    o_ref[...] = _heads_to_lanes(acc / l, n_heads, t_new)


def _block_diag_queries(qb, b, t, n_heads):
    q4 = qb.reshape(b, t, n_heads, HEAD_DIM)
    eye = jnp.eye(n_heads, dtype=qb.dtype)
    return jnp.einsum("bthd,hg->bhtgd", q4, eye).reshape(b, n_heads * t, n_heads * HEAD_DIM)


def _dsa_sample(qb, kb, vb, qib, kib2, wi, cache_k, cache_v, cache_kidx, page_table, b, t):
    dd = qb.shape[1]
    n_heads = dd // HEAD_DIM
    n_pool, page = cache_k.shape[0], cache_k.shape[1]
    n_pages = page_table.shape[1]
    past = n_pages * page
    k_sel = min(DSA_TOPK_MAX, (past + t) // 4)
    cp, n_slots = SAMPLE_PAGES_PER_CHUNK, SAMPLE_RING_SLOTS
    kcs = min(SAMPLE_SELECT_KC, past)
    assert n_pages % cp == 0 and past % kcs == 0 and t <= LANES and kcs >= k_sel
    assert n_pages % INDEXER_PAGES_PER_STEP == 0 and t == SUBLANES
    mx = MXU_DTYPE
    qblk = _block_diag_queries(qb, b, t, n_heads)
    qi2 = qib.reshape(b, t, H_IDX, D_IDX).transpose(0, 2, 1, 3).reshape(b, H_IDX * t, D_IDX)
    wcol = wi.reshape(b, t, H_IDX).transpose(0, 2, 1).reshape(b, H_IDX * t, 1)
    kinew = kib2[:, :D_IDX].reshape(b, t, D_IDX)
    per_b = lambda r, n: pl.BlockSpec((None, r, n), lambda bi, pt: (bi, 0, 0))
    anyspec = pl.BlockSpec(memory_space=pl.ANY)
    grid_spec = pltpu.PrefetchScalarGridSpec(
        num_scalar_prefetch=1, grid=(b,),
        in_specs=[per_b(n_heads * t, dd), per_b(H_IDX * t, D_IDX), per_b(H_IDX * t, 1),
                  per_b(t, dd), per_b(t, dd), per_b(t, D_IDX), anyspec, anyspec, anyspec],
        out_specs=per_b(t, dd),
        scratch_shapes=[
            pltpu.VMEM((2, n_pages, D_IDX, page), jnp.float32),
            pltpu.VMEM((t, past + kcs), jnp.int32),
            pltpu.VMEM((t, LANES), jnp.int32),
            pltpu.VMEM((n_slots, cp, n_heads, HEAD_DIM, page), jnp.float32),
            pltpu.VMEM((n_slots, cp, n_heads, HEAD_DIM, page), jnp.float32),
            pltpu.VMEM((LANES, dd), mx), pltpu.VMEM((LANES, dd), mx),
            pltpu.VMEM((LANES, D_IDX), mx),
            pltpu.SemaphoreType.DMA((2,)), pltpu.SemaphoreType.DMA((n_slots,)),
            pltpu.SemaphoreType.DMA((n_slots,)),
        ])
    out = pl.pallas_call(
        functools.partial(_dsa_sample_body, k_sel, b, n_pages, kcs),
        grid_spec=grid_spec,
        out_shape=jax.ShapeDtypeStruct((b, t, dd), jnp.float32),
        compiler_params=_cparams(("arbitrary",)), name="dsa_sample",
    )(page_table.reshape(-1), qblk, qi2, wcol, kb.reshape(b, t, dd), vb.reshape(b, t, dd),
      kinew, _pages_transposed(cache_kidx), _pages_transposed(cache_k),
      _pages_transposed(cache_v))
    return out.reshape(b * t, dd)


def _moba_sample_body(topk, n_seq, n_pages, pt_ref, qblk_ref, knew_ref, vnew_ref, ck_ref,
                      cv_ref, o_ref, s_ref, kbuf, vbuf, knew_pad, vnew_pad, sem_k, sem_v):
    b = pl.program_id(0)
    n_slots, cp = kbuf.shape[0], kbuf.shape[1]
    lookahead = n_slots - 1
    page = kbuf.shape[4]
    dm = kbuf.shape[2] * kbuf.shape[3]
    n_chunks = n_pages // cp
    n_total = n_seq * n_chunks
    ck = cp * page
    blk = MOBA_BLOCK
    bpc = ck // blk
    nb = n_pages * page // blk
    t_new = knew_ref.shape[0]
    n_heads = qblk_ref.shape[0] // t_new
    rows = n_heads * t_new

    k_copies = lambda g: _chunk_copies(ck_ref, kbuf, sem_k, pt_ref, g)
    v_copies = lambda g: _chunk_copies(cv_ref, vbuf, sem_v, pt_ref, g)
    _stream_prologue(k_copies, n_total, lookahead)
    _stream_prologue(v_copies, n_total, lookahead)
    _pad_rows(knew_pad, knew_ref)
    _pad_rows(vnew_pad, vnew_ref)
    qblk = qblk_ref[...]

    gate_lane = lax.broadcasted_iota(jnp.int32, (rows, LANES), 1)

    def k_body(j, gate):
        g_k = b * n_chunks + j
        _stream_wait(k_copies, g_k, n_total, lookahead)
        s = _dot(qblk, _chunk_transposed(kbuf, g_k % n_slots))
        s_ref[:, pl.ds(pl.multiple_of(j * ck, ck), ck)] = s
        for r in range(bpc):
            g = jnp.mean(s[:, r * blk:(r + 1) * blk], axis=1, keepdims=True)
            gate = jnp.where(gate_lane == j * bpc + r, g, gate)
        return gate

    gate = lax.fori_loop(0, n_chunks, k_body, jnp.zeros((rows, LANES), jnp.float32))
    sel = _moba_select(gate, 0, nb, topk).astype(MXU_DTYPE)
    n_row = lax.broadcasted_iota(jnp.int32, (LANES, ck), 0)
    col = lax.broadcasted_iota(jnp.int32, (LANES, ck), 1)
    blk_in_chunk = jnp.zeros((LANES, ck), jnp.int32)
    for r in range(1, bpc):
        blk_in_chunk = blk_in_chunk + jnp.where(col >= r * blk, 1, 0)

    def v_body(j, carry):
        m, l, acc = carry
        g_v = b * n_chunks + j
        _stream_wait(v_copies, g_v, n_total, lookahead)
        expand = jnp.where(n_row == j * bpc + blk_in_chunk, 1.0, 0.0).astype(MXU_DTYPE)
        s = jnp.where(_dot(sel, expand) > 0.5,
                      s_ref[:, pl.ds(pl.multiple_of(j * ck, ck), ck)], NEG_INF)
        return _softmax_step(s, m, l, acc, _chunk_transposed(vbuf, g_v % n_slots),
                             v_is_transposed=True)

    init = (jnp.full((rows, 1), M_INIT, jnp.float32), jnp.zeros((rows, 1), jnp.float32),
            jnp.zeros((rows, dm), jnp.float32))
    m, l, acc = lax.fori_loop(0, n_chunks, v_body, init)
    lane = lax.broadcasted_iota(jnp.int32, (rows, LANES), 1)
    s_new = jnp.where(lane <= _row_token(n_heads, t_new), _dot_nt(qblk, knew_pad[...]), NEG_INF)
    _, l, acc = _softmax_step(s_new, m, l, acc, vnew_pad[...])
    o_ref[...] = _heads_to_lanes(acc / l, n_heads, t_new)


def _moba_sample(qb, kb, vb, cache_k, cache_v, page_table, b, t):
    dm = qb.shape[1]
    n_heads = dm // HEAD_DIM
    n_pool, page = cache_k.shape[0], cache_k.shape[1]
    n_pages = page_table.shape[1]
    past = n_pages * page
    cp, n_slots = SAMPLE_PAGES_PER_CHUNK, SAMPLE_RING_SLOTS
    assert past % MOBA_BLOCK == 0 and t <= MOBA_BLOCK and t == SUBLANES
    assert n_pages % cp == 0 and (cp * page) % MOBA_BLOCK == 0
    nb = past // MOBA_BLOCK
    assert nb <= LANES
    topk = min(MOBA_TOPK, nb)
    mx = MXU_DTYPE
    qblk = _block_diag_queries(qb, b, t, n_heads)
    per_b = lambda r, n: pl.BlockSpec((None, r, n), lambda bi, pt: (bi, 0, 0))
    anyspec = pl.BlockSpec(memory_space=pl.ANY)
    grid_spec = pltpu.PrefetchScalarGridSpec(
        num_scalar_prefetch=1, grid=(b,),
        in_specs=[per_b(n_heads * t, dm), per_b(t, dm), per_b(t, dm), anyspec, anyspec],
        out_specs=per_b(t, dm),
        scratch_shapes=[
            pltpu.VMEM((n_heads * t, past), jnp.float32),
            pltpu.VMEM((n_slots, cp, n_heads, HEAD_DIM, page), jnp.float32),
            pltpu.VMEM((n_slots, cp, n_heads, HEAD_DIM, page), jnp.float32),
            pltpu.VMEM((LANES, dm), mx), pltpu.VMEM((LANES, dm), mx),
            pltpu.SemaphoreType.DMA((n_slots,)), pltpu.SemaphoreType.DMA((n_slots,)),
        ])
    out = pl.pallas_call(
        functools.partial(_moba_sample_body, topk, b, n_pages),
        grid_spec=grid_spec,
        out_shape=jax.ShapeDtypeStruct((b, t, dm), jnp.float32),
        compiler_params=_cparams(("arbitrary",)), name="moba_sample",
    )(page_table.reshape(-1), qblk, kb.reshape(b, t, dm), vb.reshape(b, t, dm),
      _pages_transposed(cache_k), _pages_transposed(cache_v))
    return out.reshape(b * t, dm)


def _trunk(x, p, conv_prev, ffn_prev, dsa_fn, moba_fn, w0_pad, w_conv_a0, wout0_a, wout0_b,
           w_in1, w_out1, ln_mix_g, ln_mix_b, ln_ffn_g, ln_ffn_b, w_ffn_in, w_conv_ffn,
           w_ffn_out, w_ple, w_ple_gate, alpha, c, dd, dm, is_prompt):
    b, t, d = x.shape
    m = b * t
    x2 = x.reshape(m, d)
    p2 = p.reshape(p.shape[0], m, p.shape[-1])

    ya, qb, k0, kb, v0, vb, qib, ki0, kib2, wi, conv_new = _proj0(
        x, w0_pad, w_conv_a0, conv_prev, c, dd)
    yb = dsa_fn(qb, kb, vb, qib, kib2, wi, b, t)
    x2 = _out_ln(x2, [ya, yb], [wout0_a, wout0_b], ln_mix_g[0], ln_mix_b[0], alpha)
    x2, st0 = _ffn(x2, p2[0], b, t, w_ffn_in[0], w_conv_ffn[0], w_ffn_out[0], ln_ffn_g[0],
                   ln_ffn_b[0], w_ple_gate[0], w_ple[0], ffn_prev[0], alpha)

    outs = _proj1(x2.reshape(b, t, d), w_in1, dm, is_prompt)
    qb1, k1, kb1, v1, vb1 = outs[:5]
    yc = moba_fn(qb1, kb1, vb1, *outs[5:], b, t)
    x2 = _out_ln(x2, [yc], [w_out1], ln_mix_g[1], ln_mix_b[1], alpha)
    x2, st1 = _ffn(x2, p2[1], b, t, w_ffn_in[1], w_conv_ffn[1], w_ffn_out[1], ln_ffn_g[1],
                   ln_ffn_b[1], w_ple_gate[1], w_ple[1], ffn_prev[1], alpha)

    h_dsa, h_moba = dd // HEAD_DIM, dm // HEAD_DIM
    return (x2.reshape(b, t, d), k0.reshape(b, t, h_dsa, HEAD_DIM),
            v0.reshape(b, t, h_dsa, HEAD_DIM), ki0.reshape(b, t, D_IDX), conv_new,
            k1.reshape(b, t, h_moba, HEAD_DIM), v1.reshape(b, t, h_moba, HEAD_DIM),
            jnp.stack([st0, st1]))


def kernel(x_prompt, x_sample, cache_k0, cache_v0, cache_kidx0, state_conv0, cache_k1, cache_v1, state_ffn, page_table, p_prompt, p_sample, w_in0, w_conv_a0, w_out0, w_in1, w_out1, ln_mix_g, ln_mix_b, ln_ffn_g, ln_ffn_b, w_ffn_in, w_conv_ffn, w_ffn_out, w_ple, w_ple_gate):
    depth = w_ffn_in.shape[0]
    assert depth == 2
    alpha = (2 * depth) ** 0.25
    d = x_prompt.shape[-1]
    c = w_conv_a0.shape[1]
    dd = cache_k0.shape[2] * HEAD_DIM
    dm = cache_k1.shape[2] * HEAD_DIM
    f = w_conv_ffn.shape[2]
    mx = MXU_DTYPE

    o = 3 * c + 3 * dd + H_IDX * D_IDX
    assert o % LANES == 0 and w_in0.shape[1] == o + D_IDX + H_IDX
    w0 = w_in0.astype(mx)
    ki_cols = w0[:, o:o + D_IDX]
    w0_pad = jnp.concatenate(
        [w0[:, :o], ki_cols, ki_cols, w0[:, o + D_IDX:], jnp.zeros((d, LANES - H_IDX), mx)],
        axis=1)
    wout0 = w_out0.astype(mx)
    shared = (w0_pad, w_conv_a0, wout0[:c], wout0[c:], w_in1.astype(mx), w_out1.astype(mx),
              ln_mix_g, ln_mix_b, ln_ffn_g, ln_ffn_b, w_ffn_in.astype(mx), w_conv_ffn,
              w_ffn_out.astype(mx), w_ple.astype(mx), w_ple_gate.astype(mx), alpha, c, dd, dm)

    bp = x_prompt.shape[0]
    conv_zero = jnp.zeros((bp, CONV_W - 1, c), jnp.float32)
    ffn_zero = jnp.zeros((depth, bp, CONV_W - 1, f), jnp.float32)
    prompt = _trunk(x_prompt, p_prompt, conv_zero, ffn_zero, _dsa_prompt, _moba_prompt,
                    *shared, True)

    def dsa_s(qb, kb, vb, qib, kib2, wi, b, t):
        return _dsa_sample(qb, kb, vb, qib, kib2, wi, cache_k0, cache_v0, cache_kidx0,
                           page_table, b, t)

    def moba_s(qb, kb, vb, b, t):
        return _moba_sample(qb, kb, vb, cache_k1, cache_v1, page_table, b, t)

    sample = _trunk(x_sample, p_sample, state_conv0, state_ffn, dsa_s, moba_s, *shared, False)
    y_p, k0_p, v0_p, ki_p, conv_p, k1_p, v1_p, ffn_p = prompt
    y_s, k0_s, v0_s, ki_s, conv_s, k1_s, v1_s, ffn_s = sample
    return (y_p, y_s, k0_p, v0_p, ki_p, conv_p, k1_p, v1_p, ffn_p,
            k0_s, v0_s, ki_s, conv_s, k1_s, v1_s, ffn_s)
```
